```python
import math
import jax
import jax.numpy as jnp
from jax import lax
import numpy as np

D_MODEL = 1024
BATCH = 2
SEQ = 8192
DEPTH = 2

GRID_W = 64
CTX_LEN = 256
CHUNK = 128
CONV_W = 5
RMS_EPS = 1e-6
LN_EPS = 1e-5
ROPE_BASE = 10000.0

SSD_HEADS = 16
SSD_HEAD_DIM = 64
SSD_INNER = SSD_HEADS * SSD_HEAD_DIM
SSD_GROUPS = 4
SSD_STATE = 128
SSD_CONV_DIM = SSD_INNER + 2 * SSD_GROUPS * SSD_STATE

RET_HEADS = 4
RET_QK_DIM = 128
RET_V_DIM = 256
RET_QK = RET_HEADS * RET_QK_DIM
RET_V = RET_HEADS * RET_V_DIM

AB_IN_SIZES = (SSD_INNER, SSD_CONV_DIM, SSD_HEADS, RET_QK, RET_QK, RET_V, RET_V)
AB_IN = sum(AB_IN_SIZES)
AB_SPLITS = tuple(sum(AB_IN_SIZES[:i + 1]) for i in range(len(AB_IN_SIZES) - 1))
AB_OUT = SSD_INNER + RET_V

MLSTM_HEADS = 4
MLSTM_INNER = 2 * D_MODEL
MLSTM_HEAD_DIM = MLSTM_INNER // MLSTM_HEADS
QKV_BLOCK = 4
N_QKV_BLOCKS = MLSTM_INNER // QKV_BLOCK
N_GATES = 4 * MLSTM_HEADS

FFN_HIDDEN = -(-(8 * D_MODEL) // (3 * 256)) * 256
N_EVEN = (DEPTH + 1) // 2
N_ODD = DEPTH // 2

kernel_name = 'hybrid_ssd_retention_mlstm_dit'


def rmsnorm(x, w):
    xf = x.astype(jnp.float32)
    y = xf * lax.rsqrt(jnp.mean(xf * xf, axis=-1, keepdims=True) + RMS_EPS)
    return (y * w).astype(x.dtype)


def head_layernorm(y):
    yf = y.astype(jnp.float32)
    mu = jnp.mean(yf, axis=-1, keepdims=True)
    d = yf - mu
    return d * lax.rsqrt(jnp.mean(d * d, axis=-1, keepdims=True) + LN_EPS)


def modulation(cvec, w, b):
    return jax.nn.silu(cvec) @ w + b


def modulate(hn, shift, scale):
    return hn * (1.0 + scale) + shift


def swiglu(h, w1, w3, w2):
    return (jax.nn.silu(h @ w1) * (h @ w3)) @ w2


def dwconv(x, w, b):
    ch = x.shape[-1]
    y = lax.conv_general_dilated(x, w[:, None, :].astype(x.dtype), window_strides=(1,),
                                 padding=[(CONV_W // 2, CONV_W // 2)],
                                 dimension_numbers=('NWC', 'WIO', 'NWC'), feature_group_count=ch)
    return y + b


def flip(a):
    return jnp.flip(a, axis=1)


def axial_rope(rows):
    r = jnp.repeat(jnp.arange(rows, dtype=jnp.float32), GRID_W)
    col = jnp.tile(jnp.arange(GRID_W, dtype=jnp.float32), rows)
    nf = RET_QK_DIM // 4
    inv = ROPE_BASE ** (-jnp.arange(nf, dtype=jnp.float32) / nf)
    ang = jnp.concatenate([r[:, None] * inv, col[:, None] * inv], axis=-1)
    return jnp.cos(ang), jnp.sin(ang)


def apply_rope(x, cos, sin):
    x1, x2 = jnp.split(x, 2, axis=-1)
    cc, ss = cos[None, :, None, :], sin[None, :, None, :]
    return jnp.concatenate([x1 * cc - x2 * ss, x1 * ss + x2 * cc], axis=-1).astype(x.dtype)


def to_chunks(a):
    b, n = a.shape[:2]
    return jnp.moveaxis(a.reshape(b, n // CHUNK, CHUNK, *a.shape[2:]), 1, 0)


def from_chunks(a):
    nc, b, l = a.shape[:3]
    return jnp.moveaxis(a, 0, 1).reshape(b, nc * l, *a.shape[3:])


def decay_scan(q, k, v, log_a, s0):
    tril = jnp.tril(jnp.ones((CHUNK, CHUNK), bool))[None, :, :, None]

    def body(s, inp):
        qc, kc, vc, ac = inp
        cs = jnp.cumsum(ac, axis=1)
        decay = jnp.exp(jnp.where(tril, cs[:, :, None, :] - cs[:, None, :, :], -jnp.inf))
        scores = jnp.einsum('bihd,bjhd->bijh', qc, kc) * decay
        y = (jnp.einsum('bijh,bjhe->bihe', scores, vc)
             + jnp.einsum('bihd,bhde->bihe', qc * jnp.exp(cs)[..., None], s))
        w_end = jnp.exp(cs[:, -1:, :] - cs)[..., None]
        s_new = (jnp.exp(cs[:, -1, :])[..., None, None] * s
                 + jnp.einsum('bjhd,bjhe->bhde', kc * w_end, vc))
        return s_new, y

    s_fin, ys = lax.scan(body, s0, (to_chunks(q), to_chunks(k), to_chunks(v), to_chunks(log_a)))
    return from_chunks(ys), s_fin


def bidir_decay_scan(q, k, v_f, v_b, la_f, la_b, s_f, s_b):
    y_f, s_f = decay_scan(q, k, v_f, la_f, s_f)
    y_b, s_b = decay_scan(flip(q), flip(k), flip(v_b), flip(la_b), s_b)
    return y_f + flip(y_b), s_f, s_b


def mlstm_scan(q, k, v, i_pre, log_f, state):
    tril = jnp.tril(jnp.ones((CHUNK, CHUNK), bool))[None, :, :, None]

    def body(carry, inp):
        c_mat, n_vec, m = carry
        qc, kc, vc, ic, fc = inp
        bcum = jnp.cumsum(fc, axis=1)
        d = jnp.where(tril, bcum[:, :, None, :] - bcum[:, None, :, :] + ic[:, None, :, :], -jnp.inf)
        prev = bcum + m[:, None, :]
        m_out = jnp.maximum(prev, jnp.max(d, axis=2))
        s = jnp.einsum('bihd,bjhd->bijh', qc, kc) * jnp.exp(d - m_out[:, :, None, :])
        w_prev = jnp.exp(prev - m_out)
        num = (jnp.einsum('bijh,bjhe->bihe', s, vc)
               + w_prev[..., None] * jnp.einsum('bihd,bhde->bihe', qc, c_mat))
        den = jnp.sum(s, axis=2) + w_prev * jnp.einsum('bihd,bhd->bih', qc, n_vec)
        hc = num / jnp.maximum(jnp.abs(den), jnp.exp(-m_out))[..., None]
        b_end = bcum[:, -1, :]
        d_end = b_end[:, None, :] - bcum + ic
        m_new = jnp.maximum(b_end + m, jnp.max(d_end, axis=1))
        w_k = jnp.exp(d_end - m_new[:, None, :])[..., None] * kc
        w_c = jnp.exp(b_end + m - m_new)
        c_new = w_c[..., None, None] * c_mat + jnp.einsum('bjhd,bjhe->bhde', w_k, vc)
        n_new = w_c[..., None] * n_vec + jnp.sum(w_k, axis=1)
        return (c_new, n_new, m_new), hc

    carry, hs = lax.scan(body, state, tuple(map(to_chunks, (q, k, v, i_pre, log_f))))
    return from_chunks(hs), carry


def ab_zero_states(b):
    s_ssd = jnp.zeros((b, SSD_HEADS, SSD_STATE, SSD_HEAD_DIM), jnp.float32)
    s_ret = jnp.zeros((b, RET_HEADS, RET_QK_DIM, RET_V_DIM), jnp.float32)
    return (s_ssd, s_ssd, s_ret, s_ret)


def mlstm_zero_states(b):
    st = (jnp.zeros((b, MLSTM_HEADS, MLSTM_HEAD_DIM, MLSTM_HEAD_DIM), jnp.float32),
          jnp.zeros((b, MLSTM_HEADS, MLSTM_HEAD_DIM), jnp.float32),
          jnp.zeros((b, MLSTM_HEADS), jnp.float32))
    return (st, st)


def ssd_retention_mixer(h, rope, states, in_w, conv_w, conv_b, dt_bias_f, dt_bias_b, a_log_f, a_log_b,
                        d_skip, ssd_norm_w, ret_logit_f, ret_logit_b, out_w):
    b, n, _ = h.shape
    z, xbc, dt_raw, q, k, v, g = jnp.split(h @ in_w, AB_SPLITS, axis=-1)
    xbc = jax.nn.silu(dwconv(xbc, conv_w, conv_b))
    xs, bm, cm = jnp.split(xbc, [SSD_INNER, SSD_INNER + SSD_GROUPS * SSD_STATE], axis=-1)
    xs = xs.reshape(b, n, SSD_HEADS, SSD_HEAD_DIM)
    rep = SSD_HEADS // SSD_GROUPS
    bm = jnp.repeat(bm.reshape(b, n, SSD_GROUPS, SSD_STATE), rep, axis=2)
    cm = jnp.repeat(cm.reshape(b, n, SSD_GROUPS, SSD_STATE), rep, axis=2)
    dt_raw = dt_raw.astype(jnp.float32)
    dt_f = jax.nn.softplus(dt_raw + dt_bias_f)
    dt_b = jax.nn.softplus(dt_raw + dt_bias_b)
    la_f = -dt_f * jnp.exp(a_log_f.astype(jnp.float32))
    la_b = -dt_b * jnp.exp(a_log_b.astype(jnp.float32))
    y_ssd, sf, sb = bidir_decay_scan(cm, bm, xs * dt_f[..., None], xs * dt_b[..., None],
                                     la_f, la_b, states[0], states[1])
    y_ssd = (y_ssd + d_skip[:, None] * xs).reshape(b, n, SSD_INNER) * jax.nn.silu(z)
    y_ssd = rmsnorm(y_ssd.reshape(b, n, SSD_GROUPS, SSD_INNER // SSD_GROUPS),
                    ssd_norm_w.reshape(SSD_GROUPS, SSD_INNER // SSD_GROUPS)).reshape(b, n, SSD_INNER)
    q = q.reshape(b, n, RET_HEADS, RET_QK_DIM)
    k = k.reshape(b, n, RET_HEADS, RET_QK_DIM) * RET_QK_DIM ** -0.5
    if rope is not None:
        q = apply_rope(q, *rope)
        k = apply_rope(k, *rope)
    v = v.reshape(b, n, RET_HEADS, RET_V_DIM)
    lg_f = jnp.broadcast_to(jax.nn.log_sigmoid(ret_logit_f.astype(jnp.float32)), (b, n, RET_HEADS))
    lg_b = jnp.broadcast_to(jax.nn.log_sigmoid(ret_logit_b.astype(jnp.float32)), (b, n, RET_HEADS))
    y_ret, rf, rb = bidir_decay_scan(q, k, v, v, lg_f, lg_b, states[2], states[3])
    y_ret = head_layernorm(y_ret).reshape(b, n, RET_V) * jax.nn.silu(g)
    out = jnp.concatenate([y_ssd, y_ret], axis=-1).astype(h.dtype) @ out_w
    return out, (sf, sb, rf, rb)


def mlstm_mixer(h, states, up_w, conv_w, conv_b, wq, wk, wv, gate_w, gate_b, norm_w, skip, down_w):
    b, n, _ = h.shape
    xm, z = jnp.split(h @ up_w, 2, axis=-1)
    xc = jax.nn.silu(dwconv(xm, conv_w, conv_b))

    def blockwise(a, w):
        return jnp.einsum('bnkd,kde->bnke', a.reshape(b, n, N_QKV_BLOCKS, QKV_BLOCK), w).reshape(b, n, MLSTM_INNER)

    q, k, v = blockwise(xc, wq), blockwise(xc, wk), blockwise(xm, wv)
    gates = (jnp.concatenate([q, k, v], axis=-1) @ gate_w + gate_b).astype(jnp.float32)
    i_f, f_f, i_b, f_b = jnp.split(gates, 4, axis=-1)

    def heads(a):
        return a.reshape(b, n, MLSTM_HEADS, MLSTM_HEAD_DIM)

    qh, kh, vh = heads(q), heads(k) * MLSTM_HEAD_DIM ** -0.5, heads(v)
    h_f, st_f = mlstm_scan(qh, kh, vh, i_f, jax.nn.log_sigmoid(f_f), states[0])
    h_b, st_b = mlstm_scan(flip(qh), flip(kh), flip(vh), flip(i_b), flip(jax.nn.log_sigmoid(f_b)), states[1])
    hn = head_layernorm(h_f + flip(h_b)).reshape(b, n, MLSTM_INNER) * norm_w
    y = (hn + skip * xc) * jax.nn.silu(z)
    return y.astype(h.dtype) @ down_w, (st_f, st_b)


def block_update(x, y, mod, ffn_norm_w, w1, w3, w2):
    x = x + (mod[2] * y).astype(x.dtype)
    hf = modulate(rmsnorm(x, ffn_norm_w), mod[3], mod[4])
    return x + (mod[5] * swiglu(hf, w1, w3, w2)).astype(x.dtype)


def setup_inputs(seed: int = 0) -> dict:
    key = jax.random.key(seed)
    ks = iter(jax.random.split(key, 64))

    def nrm(shape, scale):
        return scale * jax.random.normal(next(ks), shape, jnp.float32)

    def gain(shape):
        return 1.0 + nrm(shape, 0.02)

    def dt_bias():
        dt = jnp.exp(jax.random.uniform(next(ks), (N_EVEN, SSD_HEADS), jnp.float32,
                                        math.log(1e-3), math.log(1e-1)))
        return dt + jnp.log(-jnp.expm1(-dt))

    def a_log():
        return jnp.log(jax.random.uniform(next(ks), (N_EVEN, SSD_HEADS), jnp.float32, 1.0, 16.0))

    gamma0 = 1.0 - 2.0 ** (-5.0 - jnp.arange(RET_HEADS, dtype=jnp.float32))
    logit0 = jnp.log(gamma0) - jnp.log1p(-gamma0)
    f_bias0 = jnp.linspace(3.0, 6.0, MLSTM_HEADS, dtype=jnp.float32)
    D = D_MODEL
    return {
        'x': nrm((BATCH, SEQ, D), 1.0),
        'c': nrm((BATCH, D), 1.0),
        'ctx': nrm((BATCH, CTX_LEN, D), 1.0),
        'c_ctx': nrm((D,), 1.0),
        'ada_w': nrm((DEPTH, D, 6 * D), D ** -0.5),
        'ada_b': nrm((DEPTH, 6 * D), 0.02),
        'norm_mix_w': gain((DEPTH, D)),
        'norm_ffn_w': gain((DEPTH, D)),
        'ffn_w1': nrm((DEPTH, D, FFN_HIDDEN), D ** -0.5),
        'ffn_w3': nrm((DEPTH, D, FFN_HIDDEN), D ** -0.5),
        'ffn_w2': nrm((DEPTH, FFN_HIDDEN, D), FFN_HIDDEN ** -0.5),
        'ab_in_w': nrm((N_EVEN, D, AB_IN), D ** -0.5),
        'ab_conv_w': nrm((N_EVEN, CONV_W, SSD_CONV_DIM), CONV_W ** -0.5),
        'ab_conv_b': nrm((N_EVEN, SSD_CONV_DIM), 0.02),
        'ssd_dt_bias_f': dt_bias(),
        'ssd_dt_bias_b': dt_bias(),
        'ssd_a_log_f': a_log(),
        'ssd_a_log_b': a_log(),
        'ssd_d': 1.0 + nrm((N_EVEN, SSD_HEADS), 0.1),
        'ssd_norm_w': gain((N_EVEN, SSD_INNER)),
        'ret_logit_f': logit0 + nrm((N_EVEN, RET_HEADS), 0.1),
        'ret_logit_b': logit0 + nrm((N_EVEN, RET_HEADS), 0.1),
        'ab_out_w': nrm((N_EVEN, AB_OUT, D), AB_OUT ** -0.5),
        'ml_up_w': nrm((N_ODD, D, 2 * MLSTM_INNER), D ** -0.5),
        'ml_conv_w': nrm((N_ODD, CONV_W, MLSTM_INNER), CONV_W ** -0.5),
        'ml_conv_b': nrm((N_ODD, MLSTM_INNER), 0.02),
        'ml_wq': nrm((N_ODD, N_QKV_BLOCKS, QKV_BLOCK, QKV_BLOCK), QKV_BLOCK ** -0.5),
        'ml_wk': nrm((N_ODD, N_QKV_BLOCKS, QKV_BLOCK, QKV_BLOCK), QKV_BLOCK ** -0.5),
        'ml_wv': nrm((N_ODD, N_QKV_BLOCKS, QKV_BLOCK, QKV_BLOCK), QKV_BLOCK ** -0.5),
        'ml_gate_w': nrm((N_ODD, 3 * MLSTM_INNER, N_GATES), 0.1 * (3 * MLSTM_INNER) ** -0.5),
        'ml_gate_b': jnp.concatenate([nrm((N_ODD, MLSTM_HEADS), 0.1),
                                      f_bias0 + nrm((N_ODD, MLSTM_HEADS), 0.1),
                                      nrm((N_ODD, MLSTM_HEADS), 0.1),
                                      f_bias0 + nrm((N_ODD, MLSTM_HEADS), 0.1)], axis=-1),
        'ml_norm_w': gain((N_ODD, MLSTM_INNER)),
        'ml_skip': gain((N_ODD, MLSTM_INNER)),
        'ml_down_w': nrm((N_ODD, MLSTM_INNER, D), MLSTM_INNER ** -0.5),
        'final_norm_w': gain((D,)),
    }


def reference(x, c, ctx, c_ctx, ada_w, ada_b, norm_mix_w, norm_ffn_w, ffn_w1, ffn_w3, ffn_w2,
              ab_in_w, ab_conv_w, ab_conv_b, ssd_dt_bias_f, ssd_dt_bias_b, ssd_a_log_f, ssd_a_log_b,
              ssd_d, ssd_norm_w, ret_logit_f, ret_logit_b, ab_out_w,
              ml_up_w, ml_conv_w, ml_conv_b, ml_wq, ml_wk, ml_wv, ml_gate_w, ml_gate_b,
              ml_norm_w, ml_skip, ml_down_w, final_norm_w):
    b = x.shape[0]
    rows = x.shape[1] // GRID_W
    rope = axial_rope(rows)
    lat, cx = x, ctx
    for layer in range(DEPTH):
        mod_l = jnp.split(modulation(c, ada_w[layer], ada_b[layer])[:, None, :], 6, axis=-1)
        mod_c = jnp.split(modulation(c_ctx[None, :], ada_w[layer], ada_b[layer])[:, None, :], 6, axis=-1)
        h_l = modulate(rmsnorm(lat, norm_mix_w[layer]), mod_l[0], mod_l[1])
        h_c = modulate(rmsnorm(cx, norm_mix_w[layer]), mod_c[0], mod_c[1])
        j = layer // 2
        if layer % 2 == 0:
            p = (ab_in_w[j], ab_conv_w[j], ab_conv_b[j], ssd_dt_bias_f[j], ssd_dt_bias_b[j],
                 ssd_a_log_f[j], ssd_a_log_b[j], ssd_d[j], ssd_norm_w[j], ret_logit_f[j],
                 ret_logit_b[j], ab_out_w[j])
            y_c, st = ssd_retention_mixer(h_c, None, ab_zero_states(b), *p)
            y_l, _ = ssd_retention_mixer(h_l, rope, st, *p)
        else:
            p = (ml_up_w[j], ml_conv_w[j], ml_conv_b[j], ml_wq[j], ml_wk[j], ml_wv[j],
                 ml_gate_w[j], ml_gate_b[j], ml_norm_w[j], ml_skip[j], ml_down_w[j])
            y_c, st = mlstm_mixer(h_c, mlstm_zero_states(b), *p)
            y_l, _ = mlstm_mixer(h_l, st, *p)
        lat = block_update(lat, y_l, mod_l, norm_ffn_w[layer], ffn_w1[layer], ffn_w3[layer], ffn_w2[layer])
        if layer < DEPTH - 1:
            cx = block_update(cx, y_c, mod_c, norm_ffn_w[layer], ffn_w1[layer], ffn_w3[layer], ffn_w2[layer])
    return rmsnorm(lat, final_norm_w)
```

```python
import functools
import math

import jax
import jax.numpy as jnp
from jax import lax
from jax.experimental import pallas as pl
from jax.experimental.pallas import tpu as pltpu

F32 = jnp.float32
BF16 = jnp.bfloat16
HIGHEST = lax.Precision.HIGHEST

GRID_W = 64
CONV_W = 5
RMS_EPS = 1e-6
LN_EPS = 1e-5
ROPE_BASE = 10000.0

SSD_HEADS = 16
SSD_HEAD_DIM = 64
SSD_GROUPS = 4
SSD_STATE = 128
RET_HEADS = 4
RET_QK_DIM = 128
RET_V_DIM = 256
MLSTM_HEADS = 4
QKV_BLOCK = 4

LANES = 128
HALO = 16
ROW_TILE = 256
SSD_CHUNK = 128
ML_CHUNK = 256
VMEM_LIMIT = 56 * 1024 * 1024


def _cparams(sem):
    return pltpu.CompilerParams(dimension_semantics=sem, vmem_limit_bytes=VMEM_LIMIT)


def _const_spec(shape):
    nd = len(shape)
    return pl.BlockSpec(shape, lambda *_: (0,) * nd)


def _nt_dot(a, b):
    return lax.dot_general(a, b, (((1,), (1,)), ((), ())), preferred_element_type=F32)


def _tn_dot(a, b):
    return lax.dot_general(a, b, (((0,), (0,)), ((), ())), preferred_element_type=F32)


def _dot(a, b):
    return jnp.dot(a, b, preferred_element_type=F32)


def _rmsnorm(x, w):
    return x * lax.rsqrt(jnp.mean(x * x, axis=-1, keepdims=True) + RMS_EPS) * w


def _expand_heads(x, n_heads, width):
    rows = x.shape[0]
    if width >= LANES:
        return jnp.concatenate(
            [jnp.broadcast_to(x[:, h:h + 1], (rows, width)) for h in range(n_heads)], axis=1)
    per = LANES // width
    lane = lax.broadcasted_iota(jnp.int32, (rows, LANES), 1)
    pieces = []
    for k in range(n_heads // per):
        acc = jnp.broadcast_to(x[:, k * per:k * per + 1], (rows, LANES))
        for r in range(1, per):
            col = x[:, k * per + r:k * per + r + 1]
            acc = jnp.where(lane >= r * width, jnp.broadcast_to(col, (rows, LANES)), acc)
        pieces.append(acc)
    return jnp.concatenate(pieces, axis=1)


def _scan_masks(length, direction):
    ii = lax.broadcasted_iota(jnp.int32, (length, length), 0)
    jj = lax.broadcasted_iota(jnp.int32, (length, length), 1)
    rel = (ii - jj) * (1 - 2 * direction)
    return rel, rel >= 0


def _mod_kernel(cv_ref, w_ref, b_ref, o_ref):
    a = cv_ref[...]
    a = a * jax.nn.sigmoid(a)
    o_ref[0] = jnp.dot(a, w_ref[0], preferred_element_type=F32, precision=HIGHEST) + b_ref[0]


def _modulation(cvecs, ada_w, ada_b):
    depth, d, n = ada_w.shape
    tn = n // 4
    return pl.pallas_call(
        _mod_kernel,
        grid=(depth, n // tn),
        in_specs=[_const_spec(cvecs.shape),
                  pl.BlockSpec((1, d, tn), lambda l, j: (l, 0, j)),
                  pl.BlockSpec((1, 1, tn), lambda l, j: (l, 0, j))],
        out_specs=pl.BlockSpec((1, cvecs.shape[0], tn), lambda l, j: (l, 0, j)),
        out_shape=jax.ShapeDtypeStruct((depth, cvecs.shape[0], n), F32),
        compiler_params=_cparams(("parallel", "parallel")),
        name="adaln_modulation",
    )(cvecs, ada_w, ada_b.reshape(depth, 1, n))


def _proj_kernel(*refs, n_batch, n_ctx_tiles, d_model, n_out, col_chunk, split_input, with_dt):
    refs = list(refs)
    if split_input:
        ctx_ref, x_ref = refs[0], refs[1]
        refs = refs[2:]
    else:
        x_ref = refs[0]
        refs = refs[1:]
    mod_ref, nw_ref, w_ref = refs[:3]
    refs = refs[3:]
    if with_dt:
        wdt_ref, o_ref, dt_ref = refs
    else:
        (o_ref,) = refs
    b = pl.program_id(0)
    i = pl.program_id(1)
    is_ctx = i < n_ctx_tiles
    if split_input:
        xt = jnp.where(is_ctx, ctx_ref[0], x_ref[0])
    else:
        xt = x_ref[0]
    row = jnp.where(is_ctx, n_batch, b)
    shift = mod_ref[pl.ds(row, 1), 0:d_model]
    scale = mod_ref[pl.ds(row, 1), d_model:2 * d_model]
    h = _rmsnorm(xt, nw_ref[...]) * (1.0 + scale) + shift
    hb = h.astype(BF16)
    for c0 in range(0, n_out, col_chunk):
        o_ref[0, :, c0:c0 + col_chunk] = _dot(hb, w_ref[:, c0:c0 + col_chunk]).astype(BF16)
    if with_dt:
        dt_ref[0] = _dot(hb, wdt_ref[...])


def _projection(x, ctx, mod, norm_w, w, w_dt, *, n_batch, n_ctx_tiles, n_tiles, name):
    tm = ROW_TILE
    d_model, n_out = w.shape
    split = ctx is not None
    kern = functools.partial(
        _proj_kernel, n_batch=n_batch, n_ctx_tiles=n_ctx_tiles, d_model=d_model, n_out=n_out,
        col_chunk=512, split_input=split, with_dt=w_dt is not None)
    in_specs, args = [], []
    if split:
        in_specs.append(pl.BlockSpec((1, tm, d_model), lambda b, i: (b, jnp.minimum(i, n_ctx_tiles - 1), 0)))
        args.append(ctx)
        in_specs.append(pl.BlockSpec((1, tm, d_model), lambda b, i: (b, jnp.maximum(i - n_ctx_tiles, 0), 0)))
        args.append(x)
    else:
        in_specs.append(pl.BlockSpec((1, tm, d_model), lambda b, i: (b, i, 0)))
        args.append(x)
    in_specs += [_const_spec(mod.shape), _const_spec((1, d_model)), _const_spec(w.shape)]
    args += [mod, norm_w.reshape(1, d_model), w]
    out_specs = [pl.BlockSpec((1, tm, n_out), lambda b, i: (b, i, 0))]
    out_shape = [jax.ShapeDtypeStruct((n_batch, n_tiles * tm, n_out), BF16)]
    if w_dt is not None:
        in_specs.append(_const_spec(w_dt.shape))
        args.append(w_dt)
        out_specs.append(pl.BlockSpec((1, tm, LANES), lambda b, i: (b, i, 0)))
        out_shape.append(jax.ShapeDtypeStruct((n_batch, n_tiles * tm, LANES), F32))
    return pl.pallas_call(
        kern, grid=(n_batch, n_tiles), in_specs=in_specs, out_specs=out_specs, out_shape=out_shape,
        compiler_params=_cparams(("parallel", "parallel")), name=name,
    )(*args)


def _conv_silu(prev, cur, nxt, prev_ok, next_ok, cw, cb):
    tm = cur.shape[0]
    xe = jnp.concatenate([prev.astype(F32) * prev_ok, cur.astype(F32), nxt.astype(F32) * next_ok], axis=0)
    off = HALO - CONV_W // 2
    acc = cb
    for k in range(CONV_W):
        acc = acc + xe[off + k:off + k + tm, :] * cw[k:k + 1, :]
    return acc * jax.nn.sigmoid(acc)


def _halo_flags(i, n_ctx_tiles, n_tiles):
    prev_ok = jnp.where((i == 0) | (i == n_ctx_tiles), 0.0, 1.0).astype(F32)
    next_ok = jnp.where((i == n_ctx_tiles - 1) | (i == n_tiles - 1), 0.0, 1.0).astype(F32)
    return prev_ok, next_ok


def _halo_specs(tm, width, col_block, n_rows):
    r = tm // HALO
    last = n_rows // HALO - 1
    return [pl.BlockSpec((1, HALO, width), lambda b, i: (b, jnp.maximum(i * r - 1, 0), col_block)),
            pl.BlockSpec((1, tm, width), lambda b, i: (b, i, col_block)),
            pl.BlockSpec((1, HALO, width), lambda b, i: (b, jnp.minimum((i + 1) * r, last), col_block))]


def _l0_prep_kernel(prev_ref, cur_ref, next_ref, qk_ref, cc_ref, ss_ref, cw_ref, cb_ref,
                    xbc_ref, qko_ref, *, n_ctx_tiles, n_tiles, col_chunk):
    i = pl.program_id(1)
    prev_ok, next_ok = _halo_flags(i, n_ctx_tiles, n_tiles)
    width = cur_ref.shape[2]
    for c0 in range(0, width, col_chunk):
        sl = slice(c0, c0 + col_chunk)
        y = _conv_silu(prev_ref[0, :, sl], cur_ref[0, :, sl], next_ref[0, :, sl], prev_ok, next_ok,
                       cw_ref[:, sl], cb_ref[:, sl])
        xbc_ref[0, :, sl] = y.astype(BF16)
    cc = cc_ref[...]
    ss = ss_ref[...]
    n_qk = RET_HEADS * RET_QK_DIM
    for h in range(2 * RET_HEADS):
        sl = slice(h * RET_QK_DIM, (h + 1) * RET_QK_DIM)
        t = qk_ref[0, :, sl].astype(F32)
        if h * RET_QK_DIM >= n_qk:
            t = t * (RET_QK_DIM ** -0.5)
        qko_ref[0, :, sl] = (t * cc + pltpu.roll(t, RET_QK_DIM // 2, 1) * ss).astype(BF16)


def _l0_prep(p, cc, ss, conv_w, conv_b, *, n_ctx_tiles):
    n_batch, t_len, _ = p.shape
    tm = ROW_TILE
    n_tiles = t_len // tm
    c_conv = conv_w.shape[1]
    n_qk2 = 2 * RET_HEADS * RET_QK_DIM
    kern = functools.partial(_l0_prep_kernel, n_ctx_tiles=n_ctx_tiles, n_tiles=n_tiles, col_chunk=512)
    in_specs = _halo_specs(tm, c_conv, 0, t_len) + [
        pl.BlockSpec((1, tm, n_qk2), lambda b, i: (b, i, 3)),
        pl.BlockSpec((tm, LANES), lambda b, i: (i, 0)),
        pl.BlockSpec((tm, LANES), lambda b, i: (i, 0)),
        _const_spec(conv_w.shape), _const_spec((1, c_conv))]
    return pl.pallas_call(
        kern, grid=(n_batch, n_tiles), in_specs=in_specs,
        out_specs=[pl.BlockSpec((1, tm, c_conv), lambda b, i: (b, i, 0)),
                   pl.BlockSpec((1, tm, n_qk2), lambda b, i: (b, i, 0))],
        out_shape=[jax.ShapeDtypeStruct((n_batch, t_len, c_conv), BF16),
                   jax.ShapeDtypeStruct((n_batch, t_len, n_qk2), BF16)],
        compiler_params=_cparams(("parallel", "parallel")), name="l0_conv_rope",
    )(p, p, p, p, cc, ss, conv_w, conv_b.reshape(1, c_conv))


def _l0_scan_kernel(xbc_ref, dt_ref, qk_ref, v_ref, dtb_ref, alog_ref, rlog_ref, y_ref, s_ssd, s_ret):
    direction = pl.program_id(0)
    step = pl.program_id(2)
    length = xbc_ref.shape[1]

    @pl.when(step == 0)
    def _():
        s_ssd[...] = jnp.zeros_like(s_ssd)
        s_ret[...] = jnp.zeros_like(s_ret)

    rel, mask = _scan_masks(length, direction)
    maskf = mask.astype(F32)
    neg_inf = jnp.float32(-jnp.inf)

    dt = jax.nn.softplus(dt_ref[0] + dtb_ref[0])
    la = -dt * jnp.exp(alog_ref[0])
    cs = jnp.dot(maskf, la, preferred_element_type=F32, precision=HIGHEST)
    tot = jnp.sum(la, axis=0, keepdims=True)
    cs_t = cs.T
    dt_e = _expand_heads(dt, SSD_HEADS, SSD_HEAD_DIM)
    ecs_e = _expand_heads(jnp.exp(cs), SSD_HEADS, SSD_HEAD_DIM)
    wend_e = _expand_heads(jnp.exp(tot - cs), SSD_HEADS, SSD_HEAD_DIM)
    etot_e = _expand_heads(jnp.exp(tot), SSD_HEADS, SSD_HEAD_DIM)

    n_inner = SSD_HEADS * SSD_HEAD_DIM
    hpg = SSD_HEADS // SSD_GROUPS
    gw = hpg * SSD_HEAD_DIM
    head_of_lane = lax.broadcasted_iota(jnp.int32, (length, gw), 1) // SSD_HEAD_DIM
    for g in range(SSD_GROUPS):
        b_g = xbc_ref[0, :, n_inner + g * SSD_STATE:n_inner + (g + 1) * SSD_STATE]
        c0 = n_inner + SSD_GROUPS * SSD_STATE + g * SSD_STATE
        c_g = xbc_ref[0, :, c0:c0 + SSD_STATE]
        gsl = slice(g * gw, (g + 1) * gw)
        xdt = xbc_ref[0, :, gsl].astype(F32) * dt_e[:, gsl]
        scores = _nt_dot(c_g, b_g)
        m_list, x_list = [], []
        for hh in range(hpg):
            h = g * hpg + hh
            diff = cs[:, h:h + 1] - cs_t[h:h + 1, :]
            m_list.append((scores * jnp.exp(jnp.where(mask, diff, neg_inf))).astype(BF16))
            x_list.append(jnp.where(head_of_lane == hh, xdt, 0.0).astype(BF16))
        y_intra = _dot(jnp.concatenate(m_list, axis=1), jnp.concatenate(x_list, axis=0))
        state = s_ssd[g]
        y_inter = _dot(c_g, state.astype(BF16)) * ecs_e[:, gsl]
        y_ref[0, 0, :, gsl] = (y_intra + y_inter).astype(BF16)
        xw = (xdt * wend_e[:, gsl]).astype(BF16)
        s_ssd[g] = state * etot_e[:, gsl] + _tn_dot(b_g, xw)

    lg = jax.nn.log_sigmoid(rlog_ref[0])
    relf = rel.astype(F32)
    pos_i = lax.broadcasted_iota(jnp.int32, (length, 1), 0)
    pos = jnp.where(direction == 0, pos_i, length - 1 - pos_i).astype(F32)
    n_qk = RET_HEADS * RET_QK_DIM
    for h in range(RET_HEADS):
        lgh = lg[:, h:h + 1]
        q_h = qk_ref[0, :, h * RET_QK_DIM:(h + 1) * RET_QK_DIM]
        k_h = qk_ref[0, :, n_qk + h * RET_QK_DIM:n_qk + (h + 1) * RET_QK_DIM]
        v_h = v_ref[0, :, h * RET_V_DIM:(h + 1) * RET_V_DIM]
        decay = jnp.exp(jnp.where(mask, lgh * relf, neg_inf))
        m_h = (_nt_dot(q_h, k_h) * decay).astype(BF16)
        q_s = (q_h.astype(F32) * jnp.exp(lgh * (pos + 1.0))).astype(BF16)
        state = s_ret[h]
        y = _dot(jnp.concatenate([m_h, q_s], axis=1),
                 jnp.concatenate([v_h, state.astype(BF16)], axis=0))
        y_ref[0, 0, :, n_inner + h * RET_V_DIM:n_inner + (h + 1) * RET_V_DIM] = y.astype(BF16)
        k_w = (k_h.astype(F32) * jnp.exp(lgh * (length - 1.0 - pos))).astype(BF16)
        s_ret[h] = state * jnp.exp(lgh * float(length)) + _tn_dot(k_w, v_h)


def _scan_chunk_index(direction, step, n_ctx_chunks, n_chunks):
    bwd = jnp.where(step < n_ctx_chunks, n_ctx_chunks - 1 - step, n_chunks - 1 - (step - n_ctx_chunks))
    return jnp.where(direction == 0, step, bwd)


def _l0_scan(xbc, dt_raw, qk, p, dt_bias, a_log, ret_logit, *, n_ctx_chunks):
    n_batch, t_len, c_conv = xbc.shape
    length = SSD_CHUNK
    n_chunks = t_len // length
    n_v = RET_HEADS * RET_V_DIM
    n_out = SSD_HEADS * SSD_HEAD_DIM + n_v

    def tok(col_block):
        return lambda d, b, s: (b, _scan_chunk_index(d, s, n_ctx_chunks, n_chunks), col_block)

    def par():
        return pl.BlockSpec((1, 1, LANES), lambda d, b, s: (d, 0, 0))

    return pl.pallas_call(
        _l0_scan_kernel, grid=(2, n_batch, n_chunks),
        in_specs=[pl.BlockSpec((1, length, c_conv), tok(0)),
                  pl.BlockSpec((1, length, LANES), tok(0)),
                  pl.BlockSpec((1, length, qk.shape[2]), tok(0)),
                  pl.BlockSpec((1, length, n_v), tok(4)),
                  par(), par(), par()],
        out_specs=pl.BlockSpec(
            (1, 1, length, n_out),
            lambda d, b, s: (d, b, _scan_chunk_index(d, s, n_ctx_chunks, n_chunks), 0)),
        out_shape=jax.ShapeDtypeStruct((2, n_batch, t_len, n_out), BF16),
        scratch_shapes=[pltpu.VMEM((SSD_GROUPS, SSD_STATE, SSD_HEADS // SSD_GROUPS * SSD_HEAD_DIM), F32),
                        pltpu.VMEM((RET_HEADS, RET_QK_DIM, RET_V_DIM), F32)],
        compiler_params=_cparams(("arbitrary", "arbitrary", "arbitrary")), name="l0_scan",
    )(xbc, dt_raw, qk, p, dt_bias, a_log, ret_logit)


def _l0_out_kernel(yf_ref, yb_ref, xs_ref, z_ref, g_ref, ctx_ref, x_ref, mod_ref, dskip_ref, nw_ref,
                   ow_ref, o_ref, *, n_batch, n_ctx_tiles, d_model):
    b = pl.program_id(0)
    i = pl.program_id(1)
    is_ctx = i < n_ctx_tiles
    y2 = yf_ref[0, 0].astype(F32) + yb_ref[0, 0].astype(F32)
    n_inner = SSD_HEADS * SSD_HEAD_DIM
    ssd = y2[:, :n_inner] + dskip_ref[...] * xs_ref[0].astype(F32)
    z = z_ref[0].astype(F32)
    ssd = ssd * (z * jax.nn.sigmoid(z))
    gsz = n_inner // SSD_GROUPS
    pieces = []
    for g in range(SSD_GROUPS):
        sl = slice(g * gsz, (g + 1) * gsz)
        pieces.append(_rmsnorm(ssd[:, sl], nw_ref[:, sl]).astype(BF16))
    gate = g_ref[0].astype(F32)
    gate = gate * jax.nn.sigmoid(gate)
    for h in range(RET_HEADS):
        blk = y2[:, n_inner + h * RET_V_DIM:n_inner + (h + 1) * RET_V_DIM]
        dlt = blk - jnp.mean(blk, axis=-1, keepdims=True)
        nrm = dlt * lax.rsqrt(jnp.mean(dlt * dlt, axis=-1, keepdims=True) + LN_EPS)
        pieces.append((nrm * gate[:, h * RET_V_DIM:(h + 1) * RET_V_DIM]).astype(BF16))
    y = _dot(jnp.concatenate(pieces, axis=1), ow_ref[...])
    row = jnp.where(is_ctx, n_batch, b)
    gate_mix = mod_ref[pl.ds(row, 1), 2 * d_model:3 * d_model]
    xres = jnp.where(is_ctx, ctx_ref[0], x_ref[0])
    o_ref[0] = xres + gate_mix * y


def _l0_out(y, xbc, p, x, ctx, mod, d_skip_e, ssd_norm_w, out_w, *, n_ctx_tiles):
    _, n_batch, t_len, n_y = y.shape
    tm = ROW_TILE
    n_tiles = t_len // tm
    d_model = out_w.shape[1]
    n_inner = SSD_HEADS * SSD_HEAD_DIM
    kern = functools.partial(_l0_out_kernel, n_batch=n_batch, n_ctx_tiles=n_ctx_tiles, d_model=d_model)
    in_specs = [
        pl.BlockSpec((1, 1, tm, n_y), lambda b, i: (0, b, i, 0)),
        pl.BlockSpec((1, 1, tm, n_y), lambda b, i: (1, b, i, 0)),
        pl.BlockSpec((1, tm, n_inner), lambda b, i: (b, i, 0)),
        pl.BlockSpec((1, tm, n_inner), lambda b, i: (b, i, 2)),
        pl.BlockSpec((1, tm, n_inner), lambda b, i: (b, i, 5)),
        pl.BlockSpec((1, tm, d_model), lambda b, i: (b, jnp.minimum(i, n_ctx_tiles - 1), 0)),
        pl.BlockSpec((1, tm, d_model), lambda b, i: (b, jnp.maximum(i - n_ctx_tiles, 0), 0)),
        _const_spec(mod.shape), _const_spec((1, n_inner)), _const_spec((1, n_inner)),
        _const_spec(out_w.shape)]
    return pl.pallas_call(
        kern, grid=(n_batch, n_tiles), in_specs=in_specs,
        out_specs=pl.BlockSpec((1, tm, d_model), lambda b, i: (b, i, 0)),
        out_shape=jax.ShapeDtypeStruct((n_batch, t_len, d_model), F32),
        compiler_params=_cparams(("parallel", "parallel")), name="l0_out_proj",
    )(y, y, xbc, p, p, ctx, x, mod, d_skip_e.reshape(1, n_inner), ssd_norm_w.reshape(1, n_inner), out_w)


def _ffn_kernel(x_ref, mod_ref, nw_ref, w1_ref, w3_ref, w2_ref, *rest, n_batch, n_ctx_tiles, d_model,
                tile_offset, final):
    if final:
        fw_ref, o_ref = rest
    else:
        (o_ref,) = rest
    b = pl.program_id(0)
    i = pl.program_id(1) + tile_offset
    row = jnp.where(i < n_ctx_tiles, n_batch, b)
    x = x_ref[0]
    shift = mod_ref[pl.ds(row, 1), 3 * d_model:4 * d_model]
    scale = mod_ref[pl.ds(row, 1), 4 * d_model:5 * d_model]
    gate = mod_ref[pl.ds(row, 1), 5 * d_model:6 * d_model]
    hb = (_rmsnorm(x, nw_ref[...]) * (1.0 + scale) + shift).astype(BF16)
    a = _dot(hb, w1_ref[...])
    t = (a * jax.nn.sigmoid(a) * _dot(hb, w3_ref[...])).astype(BF16)
    out = x + gate * _dot(t, w2_ref[...])
    if final:
        out = _rmsnorm(out, fw_ref[...])
    o_ref[0] = out


def _ffn(x, mod, norm_w, w1, w3, w2, final_w, *, n_ctx_tiles, tile_offset):
    n_batch, t_len, d_model = x.shape
    tm = ROW_TILE
    n_tiles = t_len // tm - tile_offset
    final = final_w is not None
    kern = functools.partial(_ffn_kernel, n_batch=n_batch, n_ctx_tiles=n_ctx_tiles, d_model=d_model,
                             tile_offset=tile_offset, final=final)
    in_specs = [pl.BlockSpec((1, tm, d_model), lambda b, i: (b, i + tile_offset, 0)),
                _const_spec(mod.shape), _const_spec((1, d_model)),
                _const_spec(w1.shape), _const_spec(w3.shape), _const_spec(w2.shape)]
    args = [x, mod, norm_w.reshape(1, d_model), w1, w3, w2]
    if final:
        in_specs.append(_const_spec((1, d_model)))
        args.append(final_w.reshape(1, d_model))
    return pl.pallas_call(
        kern, grid=(n_batch, n_tiles), in_specs=in_specs,
        out_specs=pl.BlockSpec((1, tm, d_model), lambda b, i: (b, i, 0)),
        out_shape=jax.ShapeDtypeStruct((n_batch, n_tiles * tm, d_model), F32),
        compiler_params=_cparams(("parallel", "parallel")),
        name="ffn_final" if final else "ffn",
    )(*args)


def _ml_prep_kernel(prev_ref, cur_ref, next_ref, cw_ref, cb_ref, wq_ref, wk_ref, wv_ref, gw_ref, gb_ref,
                    q_ref, k_ref, v_ref, xc_ref, gates_ref, *, n_ctx_tiles, n_tiles, k_scale):
    i = pl.program_id(1)
    prev_ok, next_ok = _halo_flags(i, n_ctx_tiles, n_tiles)
    n_blocks, blk, _ = wq_ref.shape
    tm = cur_ref.shape[1]
    gates = jnp.zeros((tm, LANES), F32) + gb_ref[...]
    for c in range(n_blocks):
        sl = slice(c * blk, (c + 1) * blk)
        xm = cur_ref[0, :, sl]
        xc = _conv_silu(prev_ref[0, :, sl], xm, next_ref[0, :, sl], prev_ok, next_ok,
                        cw_ref[:, sl], cb_ref[:, sl]).astype(BF16)
        xc_ref[0, :, sl] = xc
        q = _dot(xc, wq_ref[c]).astype(BF16)
        k = _dot(xc, wk_ref[c])
        v = _dot(xm, wv_ref[c]).astype(BF16)
        q_ref[0, :, sl] = q
        k_ref[0, :, sl] = (k * k_scale).astype(BF16)
        v_ref[0, :, sl] = v
        gates = gates + _dot(q, gw_ref[0, c]) + _dot(k.astype(BF16), gw_ref[1, c]) + _dot(v, gw_ref[2, c])
    gates_ref[0] = gates


def _ml_prep(u, conv_w, conv_b, wq_bd, wk_bd, wv_bd, gate_w3, gate_b, *, n_ctx_tiles, k_scale):
    n_batch, t_len, _ = u.shape
    tm = ROW_TILE
    n_tiles = t_len // tm
    inner = conv_w.shape[1]
    kern = functools.partial(_ml_prep_kernel, n_ctx_tiles=n_ctx_tiles, n_tiles=n_tiles, k_scale=k_scale)
    tok = pl.BlockSpec((1, tm, inner), lambda b, i: (b, i, 0))
    in_specs = _halo_specs(tm, inner, 0, t_len) + [
        _const_spec(conv_w.shape), _const_spec((1, inner)),
        _const_spec(wq_bd.shape), _const_spec(wk_bd.shape), _const_spec(wv_bd.shape),
        _const_spec(gate_w3.shape), _const_spec((1, LANES))]
    act = jax.ShapeDtypeStruct((n_batch, t_len, inner), BF16)
    return pl.pallas_call(
        kern, grid=(n_batch, n_tiles), in_specs=in_specs,
        out_specs=[tok, tok, tok, tok, pl.BlockSpec((1, tm, LANES), lambda b, i: (b, i, 0))],
        out_shape=[act, act, act, act, jax.ShapeDtypeStruct((n_batch, t_len, LANES), F32)],
        compiler_params=_cparams(("parallel", "parallel")), name="ml_conv_qkv_gates",
    )(u, u, u, conv_w, conv_b.reshape(1, inner), wq_bd, wk_bd, wv_bd, gate_w3, gate_b)


def _ml_scan_kernel(q_ref, k_ref, v_ref, g_ref, h_ref, c_st, n_st, m_st):
    direction = pl.program_id(0)
    step = pl.program_id(2)
    length = q_ref.shape[1]
    dh = q_ref.shape[2] // MLSTM_HEADS

    @pl.when(step == 0)
    def _():
        c_st[...] = jnp.zeros_like(c_st)
        n_st[...] = jnp.zeros_like(n_st)
        m_st[...] = jnp.zeros_like(m_st)

    _, mask = _scan_masks(length, direction)
    maskf = mask.astype(F32)
    neg_inf = jnp.float32(-jnp.inf)
    gt = g_ref[0]
    gsel = jnp.where(direction == 0, gt, pltpu.roll(gt, LANES - 2 * MLSTM_HEADS, 1))
    lf = jax.nn.log_sigmoid(gsel)
    bcum = jnp.dot(maskf, lf, preferred_element_type=F32, precision=HIGHEST)
    btot = jnp.sum(lf, axis=0, keepdims=True)
    bcum_t = bcum.T
    ipre_t = gsel.T
    for h in range(MLSTM_HEADS):
        fl = MLSTM_HEADS + h
        bc_col = bcum[:, fl:fl + 1]
        bc_row = bcum_t[fl:fl + 1, :]
        ic_col = gsel[:, h:h + 1]
        ic_row = ipre_t[h:h + 1, :]
        m_prev = m_st[h:h + 1, 0:1]
        dmat = jnp.where(mask, bc_col - bc_row + ic_row, neg_inf)
        prev = bc_col + m_prev
        m_out = jnp.maximum(prev, jnp.max(dmat, axis=1, keepdims=True))
        sl = slice(h * dh, (h + 1) * dh)
        q_h = q_ref[0, :, sl]
        k_h = k_ref[0, :, sl]
        v_h = v_ref[0, :, sl]
        sm = _nt_dot(q_h, k_h) * jnp.exp(dmat - m_out)
        w_prev = jnp.exp(prev - m_out)
        c_prev = c_st[h]
        n_prev = n_st[h:h + 1, :]
        num = _dot(sm.astype(BF16), v_h) + w_prev * _dot(q_h, c_prev.astype(BF16))
        den = (jnp.sum(sm, axis=1, keepdims=True)
               + w_prev * jnp.sum(q_h.astype(F32) * n_prev, axis=1, keepdims=True))
        h_ref[0, 0, :, sl] = (num / jnp.maximum(jnp.abs(den), jnp.exp(-m_out))).astype(BF16)
        b_end = btot[:, fl:fl + 1]
        d_end_row = b_end - bc_row + ic_row
        m_new = jnp.maximum(b_end + m_prev, jnp.max(d_end_row, axis=1, keepdims=True))
        k_w = k_h.astype(F32) * jnp.exp(b_end - bc_col + ic_col - m_new)
        w_c = jnp.exp(b_end + m_prev - m_new)
        c_st[h] = w_c * c_prev + _tn_dot(k_w.astype(BF16), v_h)
        n_st[h:h + 1, :] = w_c * n_prev + jnp.sum(k_w, axis=0, keepdims=True)
        m_st[h:h + 1, :] = jnp.broadcast_to(m_new, (1, LANES))


def _ml_scan(q, k, v, gates, *, n_ctx_chunks):
    n_batch, t_len, inner = q.shape
    length = ML_CHUNK
    n_chunks = t_len // length
    dh = inner // MLSTM_HEADS

    def tok(d, b, s):
        return (b, _scan_chunk_index(d, s, n_ctx_chunks, n_chunks), 0)

    return pl.pallas_call(
        _ml_scan_kernel, grid=(2, n_batch, n_chunks),
        in_specs=[pl.BlockSpec((1, length, inner), tok), pl.BlockSpec((1, length, inner), tok),
                  pl.BlockSpec((1, length, inner), tok), pl.BlockSpec((1, length, LANES), tok)],
        out_specs=pl.BlockSpec(
            (1, 1, length, inner),
            lambda d, b, s: (d, b, _scan_chunk_index(d, s, n_ctx_chunks, n_chunks), 0)),
        out_shape=jax.ShapeDtypeStruct((2, n_batch, t_len, inner), BF16),
        scratch_shapes=[pltpu.VMEM((MLSTM_HEADS, dh, dh), F32), pltpu.VMEM((8, dh), F32),
                        pltpu.VMEM((8, LANES), F32)],
        compiler_params=_cparams(("arbitrary", "arbitrary", "arbitrary")), name="ml_scan",
    )(q, k, v, gates)


def _ml_out_kernel(hf_ref, hb_ref, xc_ref, z_ref, x_ref, mod_ref, nw_ref, skip_ref, dw_ref, o_ref, *, d_model):
    b = pl.program_id(0)
    hsum = hf_ref[0, 0].astype(F32) + hb_ref[0, 0].astype(F32)
    inner = hsum.shape[1]
    dh = inner // MLSTM_HEADS
    z = z_ref[0].astype(F32)
    z = z * jax.nn.sigmoid(z)
    pieces = []
    for h in range(MLSTM_HEADS):
        sl = slice(h * dh, (h + 1) * dh)
        blk = hsum[:, sl]
        dlt = blk - jnp.mean(blk, axis=-1, keepdims=True)
        nrm = dlt * lax.rsqrt(jnp.mean(dlt * dlt, axis=-1, keepdims=True) + LN_EPS) * nw_ref[:, sl]
        pieces.append(((nrm + skip_ref[:, sl] * xc_ref[0, :, sl].astype(F32)) * z[:, sl]).astype(BF16))
    y = _dot(jnp.concatenate(pieces, axis=1), dw_ref[...])
    gate_mix = mod_ref[pl.ds(b, 1), 2 * d_model:3 * d_model]
    o_ref[0] = x_ref[0] + gate_mix * y


def _ml_out(hs, xc, u, x1, mod, norm_w, skip, down_w, *, tile_offset):
    _, n_batch, t_len, inner = hs.shape
    tm = ROW_TILE
    n_tiles = t_len // tm - tile_offset
    d_model = down_w.shape[1]
    kern = functools.partial(_ml_out_kernel, d_model=d_model)
    off = tile_offset
    in_specs = [
        pl.BlockSpec((1, 1, tm, inner), lambda b, i: (0, b, i + off, 0)),
        pl.BlockSpec((1, 1, tm, inner), lambda b, i: (1, b, i + off, 0)),
        pl.BlockSpec((1, tm, inner), lambda b, i: (b, i + off, 0)),
        pl.BlockSpec((1, tm, inner), lambda b, i: (b, i + off, 1)),
        pl.BlockSpec((1, tm, d_model), lambda b, i: (b, i + off, 0)),
        _const_spec(mod.shape), _const_spec((1, inner)), _const_spec((1, inner)), _const_spec(down_w.shape)]
    return pl.pallas_call(
        kern, grid=(n_batch, n_tiles), in_specs=in_specs,
        out_specs=pl.BlockSpec((1, tm, d_model), lambda b, i: (b, i, 0)),
        out_shape=jax.ShapeDtypeStruct((n_batch, n_tiles * tm, d_model), F32),
        compiler_params=_cparams(("parallel", "parallel")), name="ml_out_proj",
    )(hs, hs, xc, u, x1, mod, norm_w.reshape(1, inner), skip.reshape(1, inner), down_w)


def _rope_tables(n_ctx, n_lat):
    rows = n_lat // GRID_W
    r = jnp.repeat(jnp.arange(rows, dtype=F32), GRID_W)
    col = jnp.tile(jnp.arange(GRID_W, dtype=F32), rows)
    nf = RET_QK_DIM // 4
    inv = ROPE_BASE ** (-jnp.arange(nf, dtype=F32) / nf)
    ang = jnp.concatenate([r[:, None] * inv, col[:, None] * inv], axis=-1)
    cos = jnp.concatenate([jnp.ones((n_ctx, 2 * nf), F32), jnp.cos(ang)], axis=0)
    sin = jnp.concatenate([jnp.zeros((n_ctx, 2 * nf), F32), jnp.sin(ang)], axis=0)
    return jnp.concatenate([cos, cos], axis=1), jnp.concatenate([-sin, sin], axis=1)


def _pad_lanes(v):
    return jnp.pad(v, [(0, 0)] * (v.ndim - 1) + [(0, LANES - v.shape[-1])])


def _block_diag(w, blocks_per_tile):
    n_k = w.shape[0]
    w4 = w.reshape(n_k // blocks_per_tile, blocks_per_tile, QKV_BLOCK, QKV_BLOCK)
    eye = jnp.eye(blocks_per_tile, dtype=w.dtype)
    side = blocks_per_tile * QKV_BLOCK
    return jnp.einsum('ckde,kl->ckdle', w4, eye).reshape(n_k // blocks_per_tile, side, side)


def kernel(x, c, ctx, c_ctx, ada_w, ada_b, norm_mix_w, norm_ffn_w, ffn_w1, ffn_w3, ffn_w2, ab_in_w, ab_conv_w, ab_conv_b, ssd_dt_bias_f, ssd_dt_bias_b, ssd_a_log_f, ssd_a_log_b, ssd_d, ssd_norm_w, ret_logit_f, ret_logit_b, ab_out_w, ml_up_w, ml_conv_w, ml_conv_b, ml_wq, ml_wk, ml_wv, ml_gate_w, ml_gate_b, ml_norm_w, ml_skip, ml_down_w, final_norm_w):
    n_batch, n_lat, d_model = x.shape
    n_ctx = ctx.shape[1]
    t_len = n_ctx + n_lat
    tm = ROW_TILE
    assert n_ctx % tm == 0 and n_lat % tm == 0 and n_ctx % ML_CHUNK == 0 and n_batch + 1 <= 8
    n_ctx_tiles = n_ctx // tm
    n_tiles = t_len // tm

    cvecs = jnp.concatenate([c, c_ctx[None, :], jnp.zeros((8 - n_batch - 1, d_model), F32)], axis=0)
    mod = _modulation(cvecs, ada_w, ada_b)

    n_inner = SSD_HEADS * SSD_HEAD_DIM
    n_conv = n_inner + 2 * SSD_GROUPS * SSD_STATE
    n_qk = RET_HEADS * RET_QK_DIM
    n_v = RET_HEADS * RET_V_DIM
    w_in = ab_in_w[0]
    o_z, o_xbc, o_dt = 0, n_inner, n_inner + n_conv
    o_q = o_dt + SSD_HEADS
    o_k, o_v, o_g = o_q + n_qk, o_q + 2 * n_qk, o_q + 2 * n_qk + n_v
    w_main = jnp.concatenate([w_in[:, o_xbc:o_dt], w_in[:, o_z:o_xbc], w_in[:, o_q:o_k], w_in[:, o_k:o_v],
                              w_in[:, o_v:o_g], w_in[:, o_g:o_g + n_v]], axis=1).astype(BF16)
    w_dt = _pad_lanes(w_in[:, o_dt:o_q]).astype(BF16)
    p, dt_raw = _projection(x, ctx, mod[0], norm_mix_w[0], w_main, w_dt, n_batch=n_batch,
                            n_ctx_tiles=n_ctx_tiles, n_tiles=n_tiles, name="l0_in_proj")
    cc, ss = _rope_tables(n_ctx, n_lat)
    xbc, qk = _l0_prep(p, cc, ss, ab_conv_w[0], ab_conv_b[0], n_ctx_tiles=n_ctx_tiles)
    dt_bias = _pad_lanes(jnp.stack([ssd_dt_bias_f[0], ssd_dt_bias_b[0]]))[:, None, :]
    a_log = _pad_lanes(jnp.stack([ssd_a_log_f[0], ssd_a_log_b[0]]))[:, None, :]
    r_logit = _pad_lanes(jnp.stack([ret_logit_f[0], ret_logit_b[0]]))[:, None, :]
    y = _l0_scan(xbc, dt_raw, qk, p, dt_bias, a_log, r_logit, n_ctx_chunks=n_ctx // SSD_CHUNK)
    d_skip_e = jnp.repeat(ssd_d[0], SSD_HEAD_DIM)
    x1 = _l0_out(y, xbc, p, x, ctx, mod[0], d_skip_e, ssd_norm_w[0], ab_out_w[0].astype(BF16),
                 n_ctx_tiles=n_ctx_tiles)
    x1 = _ffn(x1, mod[0], norm_ffn_w[0], ffn_w1[0].astype(BF16), ffn_w3[0].astype(BF16),
              ffn_w2[0].astype(BF16), None, n_ctx_tiles=n_ctx_tiles, tile_offset=0)

    inner = ml_conv_w.shape[2]
    (u,) = _projection(x1, None, mod[1], norm_mix_w[1], ml_up_w[0].astype(BF16), None, n_batch=n_batch,
                       n_ctx_tiles=n_ctx_tiles, n_tiles=n_tiles, name="ml_up_proj")
    bpt = 256 // QKV_BLOCK
    gate_w3 = _pad_lanes(ml_gate_w[0]).reshape(3, inner // 256, 256, LANES).astype(BF16)
    q, k, v, xc, gates = _ml_prep(
        u, ml_conv_w[0], ml_conv_b[0], _block_diag(ml_wq[0], bpt).astype(BF16),
        _block_diag(ml_wk[0], bpt).astype(BF16), _block_diag(ml_wv[0], bpt).astype(BF16), gate_w3,
        _pad_lanes(ml_gate_b[0])[None, :], n_ctx_tiles=n_ctx_tiles,
        k_scale=float(inner // MLSTM_HEADS) ** -0.5)
    hs = _ml_scan(q, k, v, gates, n_ctx_chunks=n_ctx // ML_CHUNK)
    x2 = _ml_out(hs, xc, u, x1, mod[1], ml_norm_w[0], ml_skip[0], ml_down_w[0].astype(BF16),
                 tile_offset=n_ctx_tiles)
    return _ffn(x2, mod[1], norm_ffn_w[1], ffn_w1[1].astype(BF16), ffn_w3[1].astype(BF16),
                ffn_w2[1].astype(BF16), final_norm_w, n_ctx_tiles=0, tile_offset=0)
```

```python
import functools
import math

import jax
import jax.numpy as jnp
from jax import lax
from jax.experimental import pallas as pl
from jax.experimental.pallas import tpu as pltpu

F32 = jnp.float32
BF16 = jnp.bfloat16
HIGHEST = lax.Precision.HIGHEST

GRID_W = 64
CONV_W = 5
RMS_EPS = 1e-6
LN_EPS = 1e-5
ROPE_BASE = 10000.0

SSD_HEADS = 16
SSD_HEAD_DIM = 64
SSD_GROUPS = 4
SSD_STATE = 128
RET_HEADS = 4
RET_QK_DIM = 128
RET_V_DIM = 256
MLSTM_HEADS = 4
QKV_BLOCK = 4

LANES = 128
HALO = 16
ROW_TILE = 256
SSD_CHUNK = 128
ML_CHUNK = 256
VMEM_LIMIT = 56 * 1024 * 1024


def _cparams(sem):
    return pltpu.CompilerParams(dimension_semantics=sem, vmem_limit_bytes=VMEM_LIMIT)


def _const_spec(shape):
    nd = len(shape)
    return pl.BlockSpec(shape, lambda *_: (0,) * nd)


def _dot(a, b):
    return jnp.dot(a, b, preferred_element_type=F32)


def _rmsnorm(x, w):
    return x * lax.rsqrt(jnp.mean(x * x, axis=-1, keepdims=True) + RMS_EPS) * w


def _scan_masks(length, direction):
    ii = lax.broadcasted_iota(jnp.int32, (length, length), 0)
    jj = lax.broadcasted_iota(jnp.int32, (length, length), 1)
    rel = (ii - jj) * (1 - 2 * direction)
    return rel, rel >= 0


def _mod_kernel(cv_ref, w_ref, b_ref, o_ref):
    a = cv_ref[...]
    a = a * jax.nn.sigmoid(a)
    o_ref[0] = jnp.dot(a, w_ref[0], preferred_element_type=F32, precision=HIGHEST) + b_ref[0]


def _modulation(cvecs, ada_w, ada_b):
    depth, d, n = ada_w.shape
    tn = n // 4
    return pl.pallas_call(
        _mod_kernel,
        grid=(depth, n // tn),
        in_specs=[_const_spec(cvecs.shape),
                  pl.BlockSpec((1, d, tn), lambda l, j: (l, 0, j)),
                  pl.BlockSpec((1, 1, tn), lambda l, j: (l, 0, j))],
        out_specs=pl.BlockSpec((1, cvecs.shape[0], tn), lambda l, j: (l, 0, j)),
        out_shape=jax.ShapeDtypeStruct((depth, cvecs.shape[0], n), F32),
        compiler_params=_cparams(("parallel", "parallel")),
        name="adaln_modulation",
    )(cvecs, ada_w, ada_b.reshape(depth, 1, n))


def _proj_kernel(*refs, n_batch, n_ctx_tiles, d_model, n_out, col_chunk, split_input, with_dt):
    refs = list(refs)
    if split_input:
        ctx_ref, x_ref = refs[0], refs[1]
        refs = refs[2:]
    else:
        x_ref = refs[0]
        refs = refs[1:]
    mod_ref, nw_ref, w_ref = refs[:3]
    refs = refs[3:]
    if with_dt:
        wdt_ref, o_ref, dt_ref = refs
    else:
        (o_ref,) = refs
    b = pl.program_id(0)
    i = pl.program_id(1)
    is_ctx = i < n_ctx_tiles
    if split_input:
        xt = jnp.where(is_ctx, ctx_ref[0], x_ref[0])
    else:
        xt = x_ref[0]
    row = jnp.where(is_ctx, n_batch, b)
    shift = mod_ref[pl.ds(row, 1), 0:d_model]
    scale = mod_ref[pl.ds(row, 1), d_model:2 * d_model]
    h = _rmsnorm(xt, nw_ref[...]) * (1.0 + scale) + shift
    hb = h.astype(BF16)
    for c0 in range(0, n_out, col_chunk):
        o_ref[0, :, c0:c0 + col_chunk] = _dot(hb, w_ref[:, c0:c0 + col_chunk]).astype(BF16)
    if with_dt:
        dt_ref[0] = _dot(hb, wdt_ref[...])


def _projection(x, ctx, mod, norm_w, w, w_dt, *, n_batch, n_ctx_tiles, n_tiles, name):
    tm = ROW_TILE
    d_model, n_out = w.shape
    split = ctx is not None
    kern = functools.partial(
        _proj_kernel, n_batch=n_batch, n_ctx_tiles=n_ctx_tiles, d_model=d_model, n_out=n_out,
        col_chunk=512, split_input=split, with_dt=w_dt is not None)
    in_specs, args = [], []
    if split:
        in_specs.append(pl.BlockSpec((1, tm, d_model), lambda b, i: (b, jnp.minimum(i, n_ctx_tiles - 1), 0)))
        args.append(ctx)
        in_specs.append(pl.BlockSpec((1, tm, d_model), lambda b, i: (b, jnp.maximum(i - n_ctx_tiles, 0), 0)))
        args.append(x)
    else:
        in_specs.append(pl.BlockSpec((1, tm, d_model), lambda b, i: (b, i, 0)))
        args.append(x)
    in_specs += [_const_spec(mod.shape), _const_spec((1, d_model)), _const_spec(w.shape)]
    args += [mod, norm_w.reshape(1, d_model), w]
    out_specs = [pl.BlockSpec((1, tm, n_out), lambda b, i: (b, i, 0))]
    out_shape = [jax.ShapeDtypeStruct((n_batch, n_tiles * tm, n_out), BF16)]
    if w_dt is not None:
        in_specs.append(_const_spec(w_dt.shape))
        args.append(w_dt)
        out_specs.append(pl.BlockSpec((1, tm, LANES), lambda b, i: (b, i, 0)))
        out_shape.append(jax.ShapeDtypeStruct((n_batch, n_tiles * tm, LANES), F32))
    return pl.pallas_call(
        kern, grid=(n_batch, n_tiles), in_specs=in_specs, out_specs=out_specs, out_shape=out_shape,
        compiler_params=_cparams(("parallel", "parallel")), name=name,
    )(*args)


def _conv_silu(prev, cur, nxt, prev_ok, next_ok, cw, cb):
    tm = cur.shape[0]
    xe = jnp.concatenate([prev.astype(F32) * prev_ok, cur.astype(F32), nxt.astype(F32) * next_ok], axis=0)
    off = HALO - CONV_W // 2
    acc = cb
    for k in range(CONV_W):
        acc = acc + xe[off + k:off + k + tm, :] * cw[k:k + 1, :]
    return acc * jax.nn.sigmoid(acc)


def _halo_flags(i, n_ctx_tiles, n_tiles):
    prev_ok = jnp.where((i == 0) | (i == n_ctx_tiles), 0.0, 1.0).astype(F32)
    next_ok = jnp.where((i == n_ctx_tiles - 1) | (i == n_tiles - 1), 0.0, 1.0).astype(F32)
    return prev_ok, next_ok


def _halo_specs(tm, width, col_block, n_rows):
    r = tm // HALO
    last = n_rows // HALO - 1
    return [pl.BlockSpec((1, HALO, width), lambda b, i: (b, jnp.maximum(i * r - 1, 0), col_block)),
            pl.BlockSpec((1, tm, width), lambda b, i: (b, i, col_block)),
            pl.BlockSpec((1, HALO, width), lambda b, i: (b, jnp.minimum((i + 1) * r, last), col_block))]


def _l0_prep_kernel(prev_ref, cur_ref, next_ref, qk_ref, cc_ref, ss_ref, cw_ref, cb_ref,
                    xc_ref, bt_ref, q_ref, kt_ref, *, n_ctx_tiles, n_tiles):
    i = pl.program_id(1)
    prev_ok, next_ok = _halo_flags(i, n_ctx_tiles, n_tiles)
    n_inner = SSD_HEADS * SSD_HEAD_DIM
    n_b = SSD_GROUPS * SSD_STATE
    width = cur_ref.shape[2]
    for c0 in range(0, width, SSD_STATE):
        sl = slice(c0, c0 + SSD_STATE)
        y = _conv_silu(prev_ref[0, :, sl], cur_ref[0, :, sl], next_ref[0, :, sl], prev_ok, next_ok,
                       cw_ref[:, sl], cb_ref[:, sl])
        if n_inner <= c0 < n_inner + n_b:
            bt_ref[0, c0 - n_inner:c0 - n_inner + SSD_STATE, :] = y.T.astype(BF16)
        else:
            dst = c0 if c0 < n_inner else c0 - n_b
            xc_ref[0, :, dst:dst + SSD_STATE] = y.astype(BF16)
    cc = cc_ref[...]
    ss = ss_ref[...]
    for h in range(2 * RET_HEADS):
        sl = slice(h * RET_QK_DIM, (h + 1) * RET_QK_DIM)
        t = qk_ref[0, :, sl].astype(F32)
        if h >= RET_HEADS:
            t = t * (RET_QK_DIM ** -0.5)
        r = t * cc + pltpu.roll(t, RET_QK_DIM // 2, 1) * ss
        if h < RET_HEADS:
            q_ref[0, :, sl] = r.astype(BF16)
        else:
            hk = h - RET_HEADS
            kt_ref[0, hk * RET_QK_DIM:(hk + 1) * RET_QK_DIM, :] = r.T.astype(BF16)


def _l0_prep(p, cc, ss, conv_w, conv_b, *, n_ctx_tiles):
    n_batch, t_len, _ = p.shape
    tm = ROW_TILE
    n_tiles = t_len // tm
    c_conv = conv_w.shape[1]
    n_b = SSD_GROUPS * SSD_STATE
    n_qk = RET_HEADS * RET_QK_DIM
    kern = functools.partial(_l0_prep_kernel, n_ctx_tiles=n_ctx_tiles, n_tiles=n_tiles)
    in_specs = _halo_specs(tm, c_conv, 0, t_len) + [
        pl.BlockSpec((1, tm, 2 * n_qk), lambda b, i: (b, i, 3)),
        pl.BlockSpec((tm, LANES), lambda b, i: (i, 0)),
        pl.BlockSpec((tm, LANES), lambda b, i: (i, 0)),
        _const_spec(conv_w.shape), _const_spec((1, c_conv))]
    return pl.pallas_call(
        kern, grid=(n_batch, n_tiles), in_specs=in_specs,
        out_specs=[pl.BlockSpec((1, tm, c_conv - n_b), lambda b, i: (b, i, 0)),
                   pl.BlockSpec((1, n_b, tm), lambda b, i: (b, 0, i)),
                   pl.BlockSpec((1, tm, n_qk), lambda b, i: (b, i, 0)),
                   pl.BlockSpec((1, n_qk, tm), lambda b, i: (b, 0, i))],
        out_shape=[jax.ShapeDtypeStruct((n_batch, t_len, c_conv - n_b), BF16),
                   jax.ShapeDtypeStruct((n_batch, n_b, t_len), BF16),
                   jax.ShapeDtypeStruct((n_batch, t_len, n_qk), BF16),
                   jax.ShapeDtypeStruct((n_batch, n_qk, t_len), BF16)],
        compiler_params=_cparams(("parallel", "parallel")), name="l0_conv_rope",
    )(p, p, p, p, cc, ss, conv_w, conv_b.reshape(1, c_conv))


def _l0_scan_kernel(xc_ref, bt_ref, dt_ref, q_ref, kt_ref, v_ref, dtb_ref, alog_ref, rlog_ref, y_ref,
                    s_ssd, s_ret):
    direction = pl.program_id(0)
    step = pl.program_id(2)
    length = xc_ref.shape[1]

    @pl.when(step == 0)
    def _():
        s_ssd[...] = jnp.zeros_like(s_ssd)
        s_ret[...] = jnp.zeros_like(s_ret)

    rel, mask = _scan_masks(length, direction)
    maskf = mask.astype(F32)
    neg_inf = jnp.float32(-jnp.inf)
    lane = lax.broadcasted_iota(jnp.int32, (length, LANES), 1)
    lane_row = lax.broadcasted_iota(jnp.int32, (1, LANES), 1)

    dt = jax.nn.softplus(dt_ref[0] + dtb_ref[0])
    la = -dt * jnp.exp(alog_ref[0])
    cs = jnp.dot(maskf, la, preferred_element_type=F32, precision=HIGHEST)
    tot = jnp.sum(la, axis=0, keepdims=True)
    etot = jnp.exp(tot)
    cs_t = cs.T
    dt_t = dt.T

    n_inner = SSD_HEADS * SSD_HEAD_DIM
    hpg = SSD_HEADS // SSD_GROUPS
    gw = hpg * SSD_HEAD_DIM
    per = LANES // SSD_HEAD_DIM
    head_of_lane = lax.broadcasted_iota(jnp.int32, (length, gw), 1) // SSD_HEAD_DIM

    def pair_up(vals, lane_idx):
        cols = []
        for k in range(hpg // per):
            acc = vals[k * per]
            for r in range(1, per):
                acc = jnp.where(lane_idx >= r * SSD_HEAD_DIM, vals[k * per + r], acc)
            cols.append(acc)
        return jnp.concatenate(cols, axis=1)

    for g in range(SSD_GROUPS):
        c_g = xc_ref[0, :, n_inner + g * SSD_STATE:n_inner + (g + 1) * SSD_STATE]
        bt_g = bt_ref[0, g * SSD_STATE:(g + 1) * SSD_STATE, :]
        bt_f = bt_g.astype(F32)
        gsl = slice(g * gw, (g + 1) * gw)
        x_g = xc_ref[0, :, gsl]
        scores = _dot(c_g, bt_g)
        m_list, w_list, e_list, t_list, x_list = [], [], [], [], []
        for hh in range(hpg):
            h = g * hpg + hh
            cs_col = jnp.broadcast_to(cs[:, h:h + 1], (length, length))
            cs_row = cs_t[h:h + 1, :]
            dt_row = dt_t[h:h + 1, :]
            decay = jnp.exp(jnp.where(mask, cs_col - cs_row, neg_inf))
            m_list.append((scores * decay * dt_row).astype(BF16))
            w_list.append((bt_f * (jnp.exp(tot[:, h:h + 1] - cs_row) * dt_row)).astype(BF16))
            e_list.append(jnp.exp(cs_col))
            t_list.append(jnp.broadcast_to(etot[:, h:h + 1], (1, LANES)))
            x_list.append(jnp.where(head_of_lane == hh, x_g, jnp.zeros_like(x_g)))
        lhs = jnp.concatenate([jnp.concatenate(m_list, axis=1), jnp.concatenate(w_list, axis=1)], axis=0)
        res = _dot(lhs, jnp.concatenate(x_list, axis=0))
        state = s_ssd[g]
        y_inter = _dot(c_g, state.astype(BF16)) * pair_up(e_list, lane)
        y_ref[0, 0, :, gsl] = (res[:length] + y_inter).astype(BF16)
        s_ssd[g] = state * pair_up(t_list, lane_row) + res[length:]

    lg = jax.nn.log_sigmoid(rlog_ref[0])
    relf = rel.astype(F32)
    pos_c = lax.broadcasted_iota(jnp.int32, (length, LANES), 0)
    pos_c = jnp.where(direction == 0, pos_c, length - 1 - pos_c).astype(F32)
    pos_r = lax.broadcasted_iota(jnp.int32, (1, length), 1)
    pos_r = jnp.where(direction == 0, pos_r, length - 1 - pos_r).astype(F32)
    for h in range(RET_HEADS):
        lgh = lg[:, h:h + 1]
        q_h = q_ref[0, :, h * RET_QK_DIM:(h + 1) * RET_QK_DIM]
        kt_h = kt_ref[0, h * RET_QK_DIM:(h + 1) * RET_QK_DIM, :]
        v_h = v_ref[0, :, h * RET_V_DIM:(h + 1) * RET_V_DIM]
        decay = jnp.exp(jnp.where(mask, lgh * relf, neg_inf))
        m_h = (_dot(q_h, kt_h) * decay).astype(BF16)
        q_s = (q_h.astype(F32) * jnp.exp(lgh * (pos_c + 1.0))).astype(BF16)
        state = s_ret[h]
        y = _dot(jnp.concatenate([m_h, q_s], axis=1),
                 jnp.concatenate([v_h, state.astype(BF16)], axis=0))
        y_ref[0, 0, :, n_inner + h * RET_V_DIM:n_inner + (h + 1) * RET_V_DIM] = y.astype(BF16)
        kt_w = (kt_h.astype(F32) * jnp.exp(lgh * (length - 1.0 - pos_r))).astype(BF16)
        s_ret[h] = state * jnp.exp(lgh * float(length)) + _dot(kt_w, v_h)


def _scan_chunk_index(direction, step, n_ctx_chunks, n_chunks):
    bwd = jnp.where(step < n_ctx_chunks, n_ctx_chunks - 1 - step, n_chunks - 1 - (step - n_ctx_chunks))
    return jnp.where(direction == 0, step, bwd)


def _l0_scan(xc, bt, dt_raw, q, kt, p, dt_bias, a_log, ret_logit, *, n_ctx_chunks):
    n_batch, t_len, c_xc = xc.shape
    length = SSD_CHUNK
    assert length == LANES
    n_chunks = t_len // length
    n_v = RET_HEADS * RET_V_DIM
    n_out = SSD_HEADS * SSD_HEAD_DIM + n_v

    def tok(col_block):
        return lambda d, b, s: (b, _scan_chunk_index(d, s, n_ctx_chunks, n_chunks), col_block)

    def tok_t(d, b, s):
        return (b, 0, _scan_chunk_index(d, s, n_ctx_chunks, n_chunks))

    def par():
        return pl.BlockSpec((1, 1, LANES), lambda d, b, s: (d, 0, 0))

    return pl.pallas_call(
        _l0_scan_kernel, grid=(2, n_batch, n_chunks),
        in_specs=[pl.BlockSpec((1, length, c_xc), tok(0)),
                  pl.BlockSpec((1, bt.shape[1], length), tok_t),
                  pl.BlockSpec((1, length, LANES), tok(0)),
                  pl.BlockSpec((1, length, q.shape[2]), tok(0)),
                  pl.BlockSpec((1, kt.shape[1], length), tok_t),
                  pl.BlockSpec((1, length, n_v), tok(4)),
                  par(), par(), par()],
        out_specs=pl.BlockSpec(
            (1, 1, length, n_out),
            lambda d, b, s: (d, b, _scan_chunk_index(d, s, n_ctx_chunks, n_chunks), 0)),
        out_shape=jax.ShapeDtypeStruct((2, n_batch, t_len, n_out), BF16),
        scratch_shapes=[pltpu.VMEM((SSD_GROUPS, SSD_STATE, SSD_HEADS // SSD_GROUPS * SSD_HEAD_DIM), F32),
                        pltpu.VMEM((RET_HEADS, RET_QK_DIM, RET_V_DIM), F32)],
        compiler_params=_cparams(("arbitrary", "arbitrary", "arbitrary")), name="l0_scan",
    )(xc, bt, dt_raw, q, kt, p, dt_bias, a_log, ret_logit)


def _l0_out_kernel(yf_ref, yb_ref, xs_ref, z_ref, g_ref, ctx_ref, x_ref, mod_ref, dskip_ref, nw_ref,
                   ow_ref, o_ref, *, n_batch, n_ctx_tiles, d_model):
    b = pl.program_id(0)
    i = pl.program_id(1)
    is_ctx = i < n_ctx_tiles
    y2 = yf_ref[0, 0].astype(F32) + yb_ref[0, 0].astype(F32)
    n_inner = SSD_HEADS * SSD_HEAD_DIM
    ssd = y2[:, :n_inner] + dskip_ref[...] * xs_ref[0].astype(F32)
    z = z_ref[0].astype(F32)
    ssd = ssd * (z * jax.nn.sigmoid(z))
    gsz = n_inner // SSD_GROUPS
    pieces = []
    for g in range(SSD_GROUPS):
        sl = slice(g * gsz, (g + 1) * gsz)
        pieces.append(_rmsnorm(ssd[:, sl], nw_ref[:, sl]).astype(BF16))
    gate = g_ref[0].astype(F32)
    gate = gate * jax.nn.sigmoid(gate)
    for h in range(RET_HEADS):
        blk = y2[:, n_inner + h * RET_V_DIM:n_inner + (h + 1) * RET_V_DIM]
        dlt = blk - jnp.mean(blk, axis=-1, keepdims=True)
        nrm = dlt * lax.rsqrt(jnp.mean(dlt * dlt, axis=-1, keepdims=True) + LN_EPS)
        pieces.append((nrm * gate[:, h * RET_V_DIM:(h + 1) * RET_V_DIM]).astype(BF16))
    y = _dot(jnp.concatenate(pieces, axis=1), ow_ref[...])
    row = jnp.where(is_ctx, n_batch, b)
    gate_mix = mod_ref[pl.ds(row, 1), 2 * d_model:3 * d_model]
    xres = jnp.where(is_ctx, ctx_ref[0], x_ref[0])
    o_ref[0] = xres + gate_mix * y


def _l0_out(y, xbc, p, x, ctx, mod, d_skip_e, ssd_norm_w, out_w, *, n_ctx_tiles):
    _, n_batch, t_len, n_y = y.shape
    tm = ROW_TILE
    n_tiles = t_len // tm
    d_model = out_w.shape[1]
    n_inner = SSD_HEADS * SSD_HEAD_DIM
    kern = functools.partial(_l0_out_kernel, n_batch=n_batch, n_ctx_tiles=n_ctx_tiles, d_model=d_model)
    in_specs = [
        pl.BlockSpec((1, 1, tm, n_y), lambda b, i: (0, b, i, 0)),
        pl.BlockSpec((1, 1, tm, n_y), lambda b, i: (1, b, i, 0)),
        pl.BlockSpec((1, tm, n_inner), lambda b, i: (b, i, 0)),
        pl.BlockSpec((1, tm, n_inner), lambda b, i: (b, i, 2)),
        pl.BlockSpec((1, tm, n_inner), lambda b, i: (b, i, 5)),
        pl.BlockSpec((1, tm, d_model), lambda b, i: (b, jnp.minimum(i, n_ctx_tiles - 1), 0)),
        pl.BlockSpec((1, tm, d_model), lambda b, i: (b, jnp.maximum(i - n_ctx_tiles, 0), 0)),
        _const_spec(mod.shape), _const_spec((1, n_inner)), _const_spec((1, n_inner)),
        _const_spec(out_w.shape)]
    return pl.pallas_call(
        kern, grid=(n_batch, n_tiles), in_specs=in_specs,
        out_specs=pl.BlockSpec((1, tm, d_model), lambda b, i: (b, i, 0)),
        out_shape=jax.ShapeDtypeStruct((n_batch, t_len, d_model), F32),
        compiler_params=_cparams(("parallel", "parallel")), name="l0_out_proj",
    )(y, y, xbc, p, p, ctx, x, mod, d_skip_e.reshape(1, n_inner), ssd_norm_w.reshape(1, n_inner), out_w)


def _ffn_kernel(x_ref, mod_ref, nw_ref, w1_ref, w3_ref, w2_ref, *rest, n_batch, n_ctx_tiles, d_model,
                tile_offset, final):
    if final:
        fw_ref, o_ref = rest
    else:
        (o_ref,) = rest
    b = pl.program_id(0)
    i = pl.program_id(1) + tile_offset
    row = jnp.where(i < n_ctx_tiles, n_batch, b)
    x = x_ref[0]
    shift = mod_ref[pl.ds(row, 1), 3 * d_model:4 * d_model]
    scale = mod_ref[pl.ds(row, 1), 4 * d_model:5 * d_model]
    gate = mod_ref[pl.ds(row, 1), 5 * d_model:6 * d_model]
    hb = (_rmsnorm(x, nw_ref[...]) * (1.0 + scale) + shift).astype(BF16)
    a = _dot(hb, w1_ref[...])
    t = (a * jax.nn.sigmoid(a) * _dot(hb, w3_ref[...])).astype(BF16)
    out = x + gate * _dot(t, w2_ref[...])
    if final:
        out = _rmsnorm(out, fw_ref[...])
    o_ref[0] = out


def _ffn(x, mod, norm_w, w1, w3, w2, final_w, *, n_ctx_tiles, tile_offset):
    n_batch, t_len, d_model = x.shape
    tm = ROW_TILE
    n_tiles = t_len // tm - tile_offset
    final = final_w is not None
    kern = functools.partial(_ffn_kernel, n_batch=n_batch, n_ctx_tiles=n_ctx_tiles, d_model=d_model,
                             tile_offset=tile_offset, final=final)
    in_specs = [pl.BlockSpec((1, tm, d_model), lambda b, i: (b, i + tile_offset, 0)),
                _const_spec(mod.shape), _const_spec((1, d_model)),
                _const_spec(w1.shape), _const_spec(w3.shape), _const_spec(w2.shape)]
    args = [x, mod, norm_w.reshape(1, d_model), w1, w3, w2]
    if final:
        in_specs.append(_const_spec((1, d_model)))
        args.append(final_w.reshape(1, d_model))
    return pl.pallas_call(
        kern, grid=(n_batch, n_tiles), in_specs=in_specs,
        out_specs=pl.BlockSpec((1, tm, d_model), lambda b, i: (b, i, 0)),
        out_shape=jax.ShapeDtypeStruct((n_batch, n_tiles * tm, d_model), F32),
        compiler_params=_cparams(("parallel", "parallel")),
        name="ffn_final" if final else "ffn",
    )(*args)


def _ml_prep_kernel(prev_ref, cur_ref, next_ref, cw_ref, cb_ref, wq_ref, wk_ref, wv_ref, gw_ref, gb_ref,
                    q_ref, k_ref, kt_ref, v_ref, xc_ref, gates_ref, *, n_ctx_tiles, n_tiles, k_scale):
    i = pl.program_id(1)
    prev_ok, next_ok = _halo_flags(i, n_ctx_tiles, n_tiles)
    n_blocks, blk, _ = wq_ref.shape
    tm = cur_ref.shape[1]
    gates = jnp.zeros((tm, LANES), F32) + gb_ref[...]
    for c in range(n_blocks):
        sl = slice(c * blk, (c + 1) * blk)
        xm = cur_ref[0, :, sl]
        xc = _conv_silu(prev_ref[0, :, sl], xm, next_ref[0, :, sl], prev_ok, next_ok,
                        cw_ref[:, sl], cb_ref[:, sl]).astype(BF16)
        xc_ref[0, :, sl] = xc
        q = _dot(xc, wq_ref[c]).astype(BF16)
        k = _dot(xc, wk_ref[c])
        v = _dot(xm, wv_ref[c]).astype(BF16)
        q_ref[0, :, sl] = q
        ks = k * k_scale
        k_ref[0, :, sl] = ks.astype(BF16)
        kt_ref[0, sl, :] = ks.T.astype(BF16)
        v_ref[0, :, sl] = v
        gates = gates + _dot(q, gw_ref[0, c]) + _dot(k.astype(BF16), gw_ref[1, c]) + _dot(v, gw_ref[2, c])
    gates_ref[0] = gates


def _ml_prep(u, conv_w, conv_b, wq_bd, wk_bd, wv_bd, gate_w3, gate_b, *, n_ctx_tiles, k_scale):
    n_batch, t_len, _ = u.shape
    tm = ROW_TILE
    n_tiles = t_len // tm
    inner = conv_w.shape[1]
    kern = functools.partial(_ml_prep_kernel, n_ctx_tiles=n_ctx_tiles, n_tiles=n_tiles, k_scale=k_scale)
    tok = pl.BlockSpec((1, tm, inner), lambda b, i: (b, i, 0))
    in_specs = _halo_specs(tm, inner, 0, t_len) + [
        _const_spec(conv_w.shape), _const_spec((1, inner)),
        _const_spec(wq_bd.shape), _const_spec(wk_bd.shape), _const_spec(wv_bd.shape),
        _const_spec(gate_w3.shape), _const_spec((1, LANES))]
    act = jax.ShapeDtypeStruct((n_batch, t_len, inner), BF16)
    return pl.pallas_call(
        kern, grid=(n_batch, n_tiles), in_specs=in_specs,
        out_specs=[tok, tok, pl.BlockSpec((1, inner, tm), lambda b, i: (b, 0, i)), tok, tok,
                   pl.BlockSpec((1, tm, LANES), lambda b, i: (b, i, 0))],
        out_shape=[act, act, jax.ShapeDtypeStruct((n_batch, inner, t_len), BF16), act, act,
                   jax.ShapeDtypeStruct((n_batch, t_len, LANES), F32)],
        compiler_params=_cparams(("parallel", "parallel")), name="ml_conv_qkv_gates",
    )(u, u, u, conv_w, conv_b.reshape(1, inner), wq_bd, wk_bd, wv_bd, gate_w3, gate_b)


def _ml_scan_kernel(q_ref, k_ref, kt_ref, v_ref, g_ref, h_ref, c_st, n_st, m_st):
    direction = pl.program_id(0)
    step = pl.program_id(2)
    length = q_ref.shape[1]
    dh = q_ref.shape[2] // MLSTM_HEADS

    @pl.when(step == 0)
    def _():
        c_st[...] = jnp.zeros_like(c_st)
        n_st[...] = jnp.zeros_like(n_st)
        m_st[...] = jnp.zeros_like(m_st)

    _, mask = _scan_masks(length, direction)
    maskf = mask.astype(F32)
    neg_inf = jnp.float32(-jnp.inf)
    gt = g_ref[0]
    gsel = jnp.where(direction == 0, gt, pltpu.roll(gt, LANES - 2 * MLSTM_HEADS, 1))
    lf = jax.nn.log_sigmoid(gsel)
    bcum = jnp.dot(maskf, lf, preferred_element_type=F32, precision=HIGHEST)
    btot = jnp.sum(lf, axis=0, keepdims=True)
    bcum_t = bcum.T
    ipre_t = gsel.T
    for h in range(MLSTM_HEADS):
        fl = MLSTM_HEADS + h
        bc_col = bcum[:, fl:fl + 1]
        bc_row = bcum_t[fl:fl + 1, :]
        ic_row = ipre_t[h:h + 1, :]
        m_prev = m_st[h:h + 1, 0:1]
        dmat = jnp.where(mask, bc_col + (ic_row - bc_row), neg_inf)
        prev = bc_col + m_prev
        m_out = jnp.maximum(prev, jnp.max(dmat, axis=1, keepdims=True))
        sl = slice(h * dh, (h + 1) * dh)
        q_h = q_ref[0, :, sl]
        k_h = k_ref[0, :, sl]
        kt_h = kt_ref[0, sl, :]
        v_h = v_ref[0, :, sl]
        sm = _dot(q_h, kt_h) * jnp.exp(dmat - m_out)
        w_prev = jnp.exp(prev - m_out)
        c_prev = c_st[h]
        n_prev = n_st[h:h + 1, :]
        num = _dot(sm.astype(BF16), v_h) + w_prev * _dot(q_h, c_prev.astype(BF16))
        den = (jnp.sum(sm, axis=1, keepdims=True)
               + w_prev * jnp.sum(q_h.astype(F32) * n_prev, axis=1, keepdims=True))
        h_ref[0, 0, :, sl] = (num / jnp.maximum(jnp.abs(den), jnp.exp(-m_out))).astype(BF16)
        b_end = btot[:, fl:fl + 1]
        d_end_row = b_end - bc_row + ic_row
        m_new = jnp.maximum(b_end + m_prev, jnp.max(d_end_row, axis=1, keepdims=True))
        w_row = jnp.exp(d_end_row - m_new)
        kt_w = (kt_h.astype(F32) * w_row).astype(BF16)
        w_c = jnp.exp(b_end + m_prev - m_new)
        c_st[h] = w_c * c_prev + _dot(kt_w, v_h)
        n_inc = _dot(jnp.broadcast_to(w_row, (8, length)).astype(BF16), k_h)
        n_st[h:h + 1, :] = w_c * n_prev + n_inc[0:1, :]
        m_st[h:h + 1, :] = jnp.broadcast_to(m_new, (1, LANES))


def _ml_scan(q, k, kt, v, gates, *, n_ctx_chunks):
    n_batch, t_len, inner = q.shape
    length = ML_CHUNK
    n_chunks = t_len // length
    dh = inner // MLSTM_HEADS

    def tok(d, b, s):
        return (b, _scan_chunk_index(d, s, n_ctx_chunks, n_chunks), 0)

    def tok_t(d, b, s):
        return (b, 0, _scan_chunk_index(d, s, n_ctx_chunks, n_chunks))

    return pl.pallas_call(
        _ml_scan_kernel, grid=(2, n_batch, n_chunks),
        in_specs=[pl.BlockSpec((1, length, inner), tok), pl.BlockSpec((1, length, inner), tok),
                  pl.BlockSpec((1, inner, length), tok_t),
                  pl.BlockSpec((1, length, inner), tok), pl.BlockSpec((1, length, LANES), tok)],
        out_specs=pl.BlockSpec(
            (1, 1, length, inner),
            lambda d, b, s: (d, b, _scan_chunk_index(d, s, n_ctx_chunks, n_chunks), 0)),
        out_shape=jax.ShapeDtypeStruct((2, n_batch, t_len, inner), BF16),
        scratch_shapes=[pltpu.VMEM((MLSTM_HEADS, dh, dh), F32), pltpu.VMEM((8, dh), F32),
                        pltpu.VMEM((8, LANES), F32)],
        compiler_params=_cparams(("arbitrary", "arbitrary", "arbitrary")), name="ml_scan",
    )(q, k, kt, v, gates)


def _ml_out_kernel(hf_ref, hb_ref, xc_ref, z_ref, x_ref, mod_ref, nw_ref, skip_ref, dw_ref, o_ref, *, d_model):
    b = pl.program_id(0)
    hsum = hf_ref[0, 0].astype(F32) + hb_ref[0, 0].astype(F32)
    inner = hsum.shape[1]
    dh = inner // MLSTM_HEADS
    z = z_ref[0].astype(F32)
    z = z * jax.nn.sigmoid(z)
    pieces = []
    for h in range(MLSTM_HEADS):
        sl = slice(h * dh, (h + 1) * dh)
        blk = hsum[:, sl]
        dlt = blk - jnp.mean(blk, axis=-1, keepdims=True)
        nrm = dlt * lax.rsqrt(jnp.mean(dlt * dlt, axis=-1, keepdims=True) + LN_EPS) * nw_ref[:, sl]
        pieces.append(((nrm + skip_ref[:, sl] * xc_ref[0, :, sl].astype(F32)) * z[:, sl]).astype(BF16))
    y = _dot(jnp.concatenate(pieces, axis=1), dw_ref[...])
    gate_mix = mod_ref[pl.ds(b, 1), 2 * d_model:3 * d_model]
    o_ref[0] = x_ref[0] + gate_mix * y


def _ml_out(hs, xc, u, x1, mod, norm_w, skip, down_w, *, tile_offset):
    _, n_batch, t_len, inner = hs.shape
    tm = ROW_TILE
    n_tiles = t_len // tm - tile_offset
    d_model = down_w.shape[1]
    kern = functools.partial(_ml_out_kernel, d_model=d_model)
    off = tile_offset
    in_specs = [
        pl.BlockSpec((1, 1, tm, inner), lambda b, i: (0, b, i + off, 0)),
        pl.BlockSpec((1, 1, tm, inner), lambda b, i: (1, b, i + off, 0)),
        pl.BlockSpec((1, tm, inner), lambda b, i: (b, i + off, 0)),
        pl.BlockSpec((1, tm, inner), lambda b, i: (b, i + off, 1)),
        pl.BlockSpec((1, tm, d_model), lambda b, i: (b, i + off, 0)),
        _const_spec(mod.shape), _const_spec((1, inner)), _const_spec((1, inner)), _const_spec(down_w.shape)]
    return pl.pallas_call(
        kern, grid=(n_batch, n_tiles), in_specs=in_specs,
        out_specs=pl.BlockSpec((1, tm, d_model), lambda b, i: (b, i, 0)),
        out_shape=jax.ShapeDtypeStruct((n_batch, n_tiles * tm, d_model), F32),
        compiler_params=_cparams(("parallel", "parallel")), name="ml_out_proj",
    )(hs, hs, xc, u, x1, mod, norm_w.reshape(1, inner), skip.reshape(1, inner), down_w)


def _rope_tables(n_ctx, n_lat):
    rows = n_lat // GRID_W
    r = jnp.repeat(jnp.arange(rows, dtype=F32), GRID_W)
    col = jnp.tile(jnp.arange(GRID_W, dtype=F32), rows)
    nf = RET_QK_DIM // 4
    inv = ROPE_BASE ** (-jnp.arange(nf, dtype=F32) / nf)
    ang = jnp.concatenate([r[:, None] * inv, col[:, None] * inv], axis=-1)
    cos = jnp.concatenate([jnp.ones((n_ctx, 2 * nf), F32), jnp.cos(ang)], axis=0)
    sin = jnp.concatenate([jnp.zeros((n_ctx, 2 * nf), F32), jnp.sin(ang)], axis=0)
    return jnp.concatenate([cos, cos], axis=1), jnp.concatenate([-sin, sin], axis=1)


def _pad_lanes(v):
    return jnp.pad(v, [(0, 0)] * (v.ndim - 1) + [(0, LANES - v.shape[-1])])


def _block_diag(w, blocks_per_tile):
    n_k = w.shape[0]
    w4 = w.reshape(n_k // blocks_per_tile, blocks_per_tile, QKV_BLOCK, QKV_BLOCK)
    eye = jnp.eye(blocks_per_tile, dtype=w.dtype)
    side = blocks_per_tile * QKV_BLOCK
    return jnp.einsum('ckde,kl->ckdle', w4, eye).reshape(n_k // blocks_per_tile, side, side)


def kernel(x, c, ctx, c_ctx, ada_w, ada_b, norm_mix_w, norm_ffn_w, ffn_w1, ffn_w3, ffn_w2, ab_in_w, ab_conv_w, ab_conv_b, ssd_dt_bias_f, ssd_dt_bias_b, ssd_a_log_f, ssd_a_log_b, ssd_d, ssd_norm_w, ret_logit_f, ret_logit_b, ab_out_w, ml_up_w, ml_conv_w, ml_conv_b, ml_wq, ml_wk, ml_wv, ml_gate_w, ml_gate_b, ml_norm_w, ml_skip, ml_down_w, final_norm_w):
    n_batch, n_lat, d_model = x.shape
    n_ctx = ctx.shape[1]
    t_len = n_ctx + n_lat
    tm = ROW_TILE
    assert n_ctx % tm == 0 and n_lat % tm == 0 and n_ctx % ML_CHUNK == 0 and n_batch + 1 <= 8
    n_ctx_tiles = n_ctx // tm
    n_tiles = t_len // tm

    cvecs = jnp.concatenate([c, c_ctx[None, :], jnp.zeros((8 - n_batch - 1, d_model), F32)], axis=0)
    mod = _modulation(cvecs, ada_w, ada_b)

    n_inner = SSD_HEADS * SSD_HEAD_DIM
    n_conv = n_inner + 2 * SSD_GROUPS * SSD_STATE
    n_qk = RET_HEADS * RET_QK_DIM
    n_v = RET_HEADS * RET_V_DIM
    w_in = ab_in_w[0]
    o_z, o_xbc, o_dt = 0, n_inner, n_inner + n_conv
    o_q = o_dt + SSD_HEADS
    o_k, o_v, o_g = o_q + n_qk, o_q + 2 * n_qk, o_q + 2 * n_qk + n_v
    w_main = jnp.concatenate([w_in[:, o_xbc:o_dt], w_in[:, o_z:o_xbc], w_in[:, o_q:o_k], w_in[:, o_k:o_v],
                              w_in[:, o_v:o_g], w_in[:, o_g:o_g + n_v]], axis=1).astype(BF16)
    w_dt = _pad_lanes(w_in[:, o_dt:o_q]).astype(BF16)
    p, dt_raw = _projection(x, ctx, mod[0], norm_mix_w[0], w_main, w_dt, n_batch=n_batch,
                            n_ctx_tiles=n_ctx_tiles, n_tiles=n_tiles, name="l0_in_proj")
    cc, ss = _rope_tables(n_ctx, n_lat)
    xcc, bt, q0, kt0 = _l0_prep(p, cc, ss, ab_conv_w[0], ab_conv_b[0], n_ctx_tiles=n_ctx_tiles)
    dt_bias = _pad_lanes(jnp.stack([ssd_dt_bias_f[0], ssd_dt_bias_b[0]]))[:, None, :]
    a_log = _pad_lanes(jnp.stack([ssd_a_log_f[0], ssd_a_log_b[0]]))[:, None, :]
    r_logit = _pad_lanes(jnp.stack([ret_logit_f[0], ret_logit_b[0]]))[:, None, :]
    y = _l0_scan(xcc, bt, dt_raw, q0, kt0, p, dt_bias, a_log, r_logit, n_ctx_chunks=n_ctx // SSD_CHUNK)
    d_skip_e = jnp.repeat(ssd_d[0], SSD_HEAD_DIM)
    x1 = _l0_out(y, xcc, p, x, ctx, mod[0], d_skip_e, ssd_norm_w[0], ab_out_w[0].astype(BF16),
                 n_ctx_tiles=n_ctx_tiles)
    x1 = _ffn(x1, mod[0], norm_ffn_w[0], ffn_w1[0].astype(BF16), ffn_w3[0].astype(BF16),
              ffn_w2[0].astype(BF16), None, n_ctx_tiles=n_ctx_tiles, tile_offset=0)

    inner = ml_conv_w.shape[2]
    (u,) = _projection(x1, None, mod[1], norm_mix_w[1], ml_up_w[0].astype(BF16), None, n_batch=n_batch,
                       n_ctx_tiles=n_ctx_tiles, n_tiles=n_tiles, name="ml_up_proj")
    bpt = 256 // QKV_BLOCK
    gate_w3 = _pad_lanes(ml_gate_w[0]).reshape(3, inner // 256, 256, LANES).astype(BF16)
    q, k, kt, v, xc, gates = _ml_prep(
        u, ml_conv_w[0], ml_conv_b[0], _block_diag(ml_wq[0], bpt).astype(BF16),
        _block_diag(ml_wk[0], bpt).astype(BF16), _block_diag(ml_wv[0], bpt).astype(BF16), gate_w3,
        _pad_lanes(ml_gate_b[0])[None, :], n_ctx_tiles=n_ctx_tiles,
        k_scale=float(inner // MLSTM_HEADS) ** -0.5)
    hs = _ml_scan(q, k, kt, v, gates, n_ctx_chunks=n_ctx // ML_CHUNK)
    x2 = _ml_out(hs, xc, u, x1, mod[1], ml_norm_w[0], ml_skip[0], ml_down_w[0].astype(BF16),
                 tile_offset=n_ctx_tiles)
    return _ffn(x2, mod[1], norm_ffn_w[1], ffn_w1[1].astype(BF16), ffn_w3[1].astype(BF16),
                ffn_w2[1].astype(BF16), final_norm_w, n_ctx_tiles=0, tile_offset=0)
```

```python
import functools
import math

import jax
import jax.numpy as jnp
from jax import lax
from jax.experimental import pallas as pl
from jax.experimental.pallas import tpu as pltpu

F32 = jnp.float32
BF16 = jnp.bfloat16
HIGHEST = lax.Precision.HIGHEST

GRID_W = 64
CONV_W = 5
RMS_EPS = 1e-6
LN_EPS = 1e-5
ROPE_BASE = 10000.0
LOG2E = math.log2(math.e)

SSD_HEADS = 16
SSD_HEAD_DIM = 64
SSD_GROUPS = 4
SSD_STATE = 128
RET_HEADS = 4
RET_QK_DIM = 128
RET_V_DIM = 256
MLSTM_HEADS = 4
QKV_BLOCK = 4

LANES = 128
HALO = 16
ROW_TILE = 256
SSD_CHUNK = 128
ML_CHUNK = 256
VMEM_LIMIT = 56 * 1024 * 1024


def _cparams(sem):
    return pltpu.CompilerParams(dimension_semantics=sem, vmem_limit_bytes=VMEM_LIMIT)


def _const_spec(shape):
    nd = len(shape)
    return pl.BlockSpec(shape, lambda *_: (0,) * nd, pipeline_mode=pl.Buffered(1))


def _dot(a, b):
    return jnp.dot(a, b, preferred_element_type=F32)


def _rmsnorm(x, w):
    return x * lax.rsqrt(jnp.mean(x * x, axis=-1, keepdims=True) + RMS_EPS) * w


def _scan_masks(length, direction):
    ii = lax.broadcasted_iota(jnp.int32, (length, length), 0)
    jj = lax.broadcasted_iota(jnp.int32, (length, length), 1)
    rel = (ii - jj) * (1 - 2 * direction)
    return rel, rel >= 0


def _mod_kernel(cv_ref, w_ref, b_ref, o_ref):
    a = cv_ref[...]
    a = a * jax.nn.sigmoid(a)
    o_ref[0] = jnp.dot(a, w_ref[0], preferred_element_type=F32, precision=HIGHEST) + b_ref[0]


def _modulation(cvecs, ada_w, ada_b):
    depth, d, n = ada_w.shape
    tn = n // 4
    return pl.pallas_call(
        _mod_kernel,
        grid=(depth, n // tn),
        in_specs=[_const_spec(cvecs.shape),
                  pl.BlockSpec((1, d, tn), lambda l, j: (l, 0, j)),
                  pl.BlockSpec((1, 1, tn), lambda l, j: (l, 0, j))],
        out_specs=pl.BlockSpec((1, cvecs.shape[0], tn), lambda l, j: (l, 0, j)),
        out_shape=jax.ShapeDtypeStruct((depth, cvecs.shape[0], n), F32),
        compiler_params=_cparams(("parallel", "parallel")),
        name="adaln_modulation",
    )(cvecs, ada_w, ada_b.reshape(depth, 1, n))


def _proj_kernel(*refs, n_batch, n_ctx_tiles, d_model, n_out, col_chunk, split_input, with_dt):
    refs = list(refs)
    if split_input:
        ctx_ref, x_ref = refs[0], refs[1]
        refs = refs[2:]
    else:
        x_ref = refs[0]
        refs = refs[1:]
    mod_ref, nw_ref, w_ref = refs[:3]
    refs = refs[3:]
    if with_dt:
        wdt_ref, o_ref, dt_ref = refs
    else:
        (o_ref,) = refs
    b = pl.program_id(0)
    i = pl.program_id(1)
    is_ctx = i < n_ctx_tiles
    if split_input:
        xt = jnp.where(is_ctx, ctx_ref[0], x_ref[0])
    else:
        xt = x_ref[0]
    row = jnp.where(is_ctx, n_batch, b)
    shift = mod_ref[pl.ds(row, 1), 0:d_model]
    scale = mod_ref[pl.ds(row, 1), d_model:2 * d_model]
    h = _rmsnorm(xt, nw_ref[...]) * (1.0 + scale) + shift
    hb = h.astype(BF16)
    for c0 in range(0, n_out, col_chunk):
        o_ref[0, :, c0:c0 + col_chunk] = _dot(hb, w_ref[:, c0:c0 + col_chunk]).astype(BF16)
    if with_dt:
        dt_ref[0] = _dot(hb, wdt_ref[...])


def _projection(x, ctx, mod, norm_w, w, w_dt, *, n_batch, n_ctx_tiles, n_tiles, name):
    tm = ROW_TILE
    d_model, n_out = w.shape
    split = ctx is not None
    kern = functools.partial(
        _proj_kernel, n_batch=n_batch, n_ctx_tiles=n_ctx_tiles, d_model=d_model, n_out=n_out,
        col_chunk=512, split_input=split, with_dt=w_dt is not None)
    in_specs, args = [], []
    if split:
        in_specs.append(pl.BlockSpec((1, tm, d_model), lambda b, i: (b, jnp.minimum(i, n_ctx_tiles - 1), 0)))
        args.append(ctx)
        in_specs.append(pl.BlockSpec((1, tm, d_model), lambda b, i: (b, jnp.maximum(i - n_ctx_tiles, 0), 0)))
        args.append(x)
    else:
        in_specs.append(pl.BlockSpec((1, tm, d_model), lambda b, i: (b, i, 0)))
        args.append(x)
    in_specs += [_const_spec(mod.shape), _const_spec((1, d_model)), _const_spec(w.shape)]
    args += [mod, norm_w.reshape(1, d_model), w]
    out_specs = [pl.BlockSpec((1, tm, n_out), lambda b, i: (b, i, 0))]
    out_shape = [jax.ShapeDtypeStruct((n_batch, n_tiles * tm, n_out), BF16)]
    if w_dt is not None:
        in_specs.append(_const_spec(w_dt.shape))
        args.append(w_dt)
        out_specs.append(pl.BlockSpec((1, tm, LANES), lambda b, i: (b, i, 0)))
        out_shape.append(jax.ShapeDtypeStruct((n_batch, n_tiles * tm, LANES), F32))
    return pl.pallas_call(
        kern, grid=(n_batch, n_tiles), in_specs=in_specs, out_specs=out_specs, out_shape=out_shape,
        compiler_params=_cparams(("parallel", "parallel")), name=name,
    )(*args)


def _conv_silu(prev, cur, nxt, prev_ok, next_ok, cw, cb):
    tm = cur.shape[0]
    xe = jnp.concatenate([prev.astype(F32) * prev_ok, cur.astype(F32), nxt.astype(F32) * next_ok], axis=0)
    off = HALO - CONV_W // 2
    acc = cb
    for k in range(CONV_W):
        acc = acc + xe[off + k:off + k + tm, :] * cw[k:k + 1, :]
    return acc * jax.nn.sigmoid(acc)


def _halo_flags(i, n_ctx_tiles, n_tiles):
    prev_ok = jnp.where((i == 0) | (i == n_ctx_tiles), 0.0, 1.0).astype(F32)
    next_ok = jnp.where((i == n_ctx_tiles - 1) | (i == n_tiles - 1), 0.0, 1.0).astype(F32)
    return prev_ok, next_ok


def _halo_specs(tm, width, col_block, n_rows):
    r = tm // HALO
    last = n_rows // HALO - 1
    return [pl.BlockSpec((1, HALO, width), lambda b, i: (b, jnp.maximum(i * r - 1, 0), col_block)),
            pl.BlockSpec((1, tm, width), lambda b, i: (b, i, col_block)),
            pl.BlockSpec((1, HALO, width), lambda b, i: (b, jnp.minimum((i + 1) * r, last), col_block))]


def _l0_prep_kernel(prev_ref, cur_ref, next_ref, qk_ref, cc_ref, ss_ref, cw_ref, cb_ref,
                    xc_ref, bt_ref, q_ref, kt_ref, *, n_ctx_tiles, n_tiles):
    i = pl.program_id(1)
    prev_ok, next_ok = _halo_flags(i, n_ctx_tiles, n_tiles)
    n_inner = SSD_HEADS * SSD_HEAD_DIM
    n_b = SSD_GROUPS * SSD_STATE
    width = cur_ref.shape[2]
    for c0 in range(0, width, SSD_STATE):
        sl = slice(c0, c0 + SSD_STATE)
        y = _conv_silu(prev_ref[0, :, sl], cur_ref[0, :, sl], next_ref[0, :, sl], prev_ok, next_ok,
                       cw_ref[:, sl], cb_ref[:, sl])
        if n_inner <= c0 < n_inner + n_b:
            bt_ref[0, c0 - n_inner:c0 - n_inner + SSD_STATE, :] = y.T.astype(BF16)
        else:
            dst = c0 if c0 < n_inner else c0 - n_b
            xc_ref[0, :, dst:dst + SSD_STATE] = y.astype(BF16)
    cc = cc_ref[...]
    ss = ss_ref[...]
    for h in range(2 * RET_HEADS):
        sl = slice(h * RET_QK_DIM, (h + 1) * RET_QK_DIM)
        t = qk_ref[0, :, sl].astype(F32)
        if h >= RET_HEADS:
            t = t * (RET_QK_DIM ** -0.5)
        r = t * cc + pltpu.roll(t, RET_QK_DIM // 2, 1) * ss
        if h < RET_HEADS:
            q_ref[0, :, sl] = r.astype(BF16)
        else:
            hk = h - RET_HEADS
            kt_ref[0, hk * RET_QK_DIM:(hk + 1) * RET_QK_DIM, :] = r.T.astype(BF16)


def _l0_prep(p, cc, ss, conv_w, conv_b, *, n_ctx_tiles):
    n_batch, t_len, _ = p.shape
    tm = ROW_TILE
    n_tiles = t_len // tm
    c_conv = conv_w.shape[1]
    n_b = SSD_GROUPS * SSD_STATE
    n_qk = RET_HEADS * RET_QK_DIM
    kern = functools.partial(_l0_prep_kernel, n_ctx_tiles=n_ctx_tiles, n_tiles=n_tiles)
    in_specs = _halo_specs(tm, c_conv, 0, t_len) + [
        pl.BlockSpec((1, tm, 2 * n_qk), lambda b, i: (b, i, 3)),
        pl.BlockSpec((tm, LANES), lambda b, i: (i, 0)),
        pl.BlockSpec((tm, LANES), lambda b, i: (i, 0)),
        _const_spec(conv_w.shape), _const_spec((1, c_conv))]
    return pl.pallas_call(
        kern, grid=(n_batch, n_tiles), in_specs=in_specs,
        out_specs=[pl.BlockSpec((1, tm, c_conv - n_b), lambda b, i: (b, i, 0)),
                   pl.BlockSpec((1, n_b, tm), lambda b, i: (b, 0, i)),
                   pl.BlockSpec((1, tm, n_qk), lambda b, i: (b, i, 0)),
                   pl.BlockSpec((1, n_qk, tm), lambda b, i: (b, 0, i))],
        out_shape=[jax.ShapeDtypeStruct((n_batch, t_len, c_conv - n_b), BF16),
                   jax.ShapeDtypeStruct((n_batch, n_b, t_len), BF16),
                   jax.ShapeDtypeStruct((n_batch, t_len, n_qk), BF16),
                   jax.ShapeDtypeStruct((n_batch, n_qk, t_len), BF16)],
        compiler_params=_cparams(("parallel", "parallel")), name="l0_conv_rope",
    )(p, p, p, p, cc, ss, conv_w, conv_b.reshape(1, c_conv))


def _l0_scan_direction(direction, xc_ref, bt_ref, dt_ref, q_ref, kt_ref, v_ref, dtb_ref, alog_ref, rlog_ref,
                       y_ref, s_ssd, s_ret):
    length = xc_ref.shape[1]
    rel, mask = _scan_masks(length, direction)
    maskf = mask.astype(F32)
    neg_inf = jnp.float32(-jnp.inf)
    lane = lax.broadcasted_iota(jnp.int32, (length, LANES), 1)
    lane_row = lax.broadcasted_iota(jnp.int32, (1, LANES), 1)

    dt = jax.nn.softplus(dt_ref[0] + dtb_ref[direction])
    la = -dt * jnp.exp(alog_ref[direction])
    cs = jnp.dot(maskf, la, preferred_element_type=F32, precision=HIGHEST)
    tot = jnp.sum(la, axis=0, keepdims=True)
    etot = jnp.exp(tot)
    cs_t = cs.T
    dt_t = dt.T

    n_inner = SSD_HEADS * SSD_HEAD_DIM
    hpg = SSD_HEADS // SSD_GROUPS
    gw = hpg * SSD_HEAD_DIM
    per = LANES // SSD_HEAD_DIM
    head_of_lane = lax.broadcasted_iota(jnp.int32, (length, gw), 1) // SSD_HEAD_DIM

    def pair_up(vals, lane_idx):
        cols = []
        for k in range(hpg // per):
            acc = vals[k * per]
            for r in range(1, per):
                acc = jnp.where(lane_idx >= r * SSD_HEAD_DIM, vals[k * per + r], acc)
            cols.append(acc)
        return jnp.concatenate(cols, axis=1)

    for g in range(SSD_GROUPS):
        c_g = xc_ref[0, :, n_inner + g * SSD_STATE:n_inner + (g + 1) * SSD_STATE]
        bt_g = bt_ref[0, g * SSD_STATE:(g + 1) * SSD_STATE, :]
        bt_f = bt_g.astype(F32)
        gsl = slice(g * gw, (g + 1) * gw)
        x_g = xc_ref[0, :, gsl]
        scores = _dot(c_g, bt_g)
        m_list, w_list, e_list, t_list, x_list = [], [], [], [], []
        for hh in range(hpg):
            h = g * hpg + hh
            cs_col = jnp.broadcast_to(cs[:, h:h + 1], (length, length))
            cs_row = cs_t[h:h + 1, :]
            dt_row = dt_t[h:h + 1, :]
            decay = jnp.exp(jnp.where(mask, cs_col - cs_row, neg_inf))
            m_list.append((scores * decay * dt_row).astype(BF16))
            w_list.append((bt_f * (jnp.exp(tot[:, h:h + 1] - cs_row) * dt_row)).astype(BF16))
            e_list.append(jnp.exp(cs_col))
            t_list.append(jnp.broadcast_to(etot[:, h:h + 1], (1, LANES)))
            x_list.append(jnp.where(head_of_lane == hh, x_g, jnp.zeros_like(x_g)))
        lhs = jnp.concatenate([jnp.concatenate(m_list, axis=1), jnp.concatenate(w_list, axis=1)], axis=0)
        res = _dot(lhs, jnp.concatenate(x_list, axis=0))
        slot = direction * SSD_GROUPS + g
        state = s_ssd[slot]
        y_inter = _dot(c_g, state.astype(BF16)) * pair_up(e_list, lane)
        y_ref[0, :, gsl] = (res[:length] + y_inter).astype(BF16)
        s_ssd[slot] = state * pair_up(t_list, lane_row) + res[length:]

    lg = jax.nn.log_sigmoid(rlog_ref[direction])
    relf = rel.astype(F32)
    pos_c = lax.broadcasted_iota(jnp.int32, (length, LANES), 0)
    pos_c = (pos_c if direction == 0 else length - 1 - pos_c).astype(F32)
    pos_r = lax.broadcasted_iota(jnp.int32, (1, length), 1)
    pos_r = (pos_r if direction == 0 else length - 1 - pos_r).astype(F32)
    for h in range(RET_HEADS):
        lgh = lg[:, h:h + 1]
        q_h = q_ref[0, :, h * RET_QK_DIM:(h + 1) * RET_QK_DIM]
        kt_h = kt_ref[0, h * RET_QK_DIM:(h + 1) * RET_QK_DIM, :]
        v_h = v_ref[0, :, h * RET_V_DIM:(h + 1) * RET_V_DIM]
        decay = jnp.exp(jnp.where(mask, lgh * relf, neg_inf))
        m_h = (_dot(q_h, kt_h) * decay).astype(BF16)
        q_s = (q_h.astype(F32) * jnp.exp(lgh * (pos_c + 1.0))).astype(BF16)
        slot = direction * RET_HEADS + h
        state = s_ret[slot]
        y = _dot(jnp.concatenate([m_h, q_s], axis=1),
                 jnp.concatenate([v_h, state.astype(BF16)], axis=0))
        y_ref[0, :, n_inner + h * RET_V_DIM:n_inner + (h + 1) * RET_V_DIM] = y.astype(BF16)
        kt_w = (kt_h.astype(F32) * jnp.exp(lgh * (length - 1.0 - pos_r))).astype(BF16)
        s_ret[slot] = state * jnp.exp(lgh * float(length)) + _dot(kt_w, v_h)


def _l0_scan_kernel(xcf, btf, dtf, qf, ktf, vf, xcb, btb, dtb, qb, ktb, vb, dtb_ref, alog_ref, rlog_ref,
                    yf_ref, yb_ref, s_ssd, s_ret):
    @pl.when(pl.program_id(1) == 0)
    def _():
        s_ssd[...] = jnp.zeros_like(s_ssd)
        s_ret[...] = jnp.zeros_like(s_ret)

    _l0_scan_direction(0, xcf, btf, dtf, qf, ktf, vf, dtb_ref, alog_ref, rlog_ref, yf_ref, s_ssd, s_ret)
    _l0_scan_direction(1, xcb, btb, dtb, qb, ktb, vb, dtb_ref, alog_ref, rlog_ref, yb_ref, s_ssd, s_ret)


def _scan_chunk_index(direction, step, n_ctx_chunks, n_chunks):
    bwd = jnp.where(step < n_ctx_chunks, n_ctx_chunks - 1 - step, n_chunks - 1 - (step - n_ctx_chunks))
    return jnp.where(direction == 0, step, bwd)


def _l0_scan(xc, bt, dt_raw, q, kt, p, dt_bias, a_log, ret_logit, *, n_ctx_chunks):
    n_batch, t_len, c_xc = xc.shape
    length = SSD_CHUNK
    assert length == LANES
    n_chunks = t_len // length
    n_v = RET_HEADS * RET_V_DIM
    n_out = SSD_HEADS * SSD_HEAD_DIM + n_v

    def specs(direction):
        def tok(col_block):
            return lambda b, s: (b, _scan_chunk_index(direction, s, n_ctx_chunks, n_chunks), col_block)

        def tok_t(b, s):
            return (b, 0, _scan_chunk_index(direction, s, n_ctx_chunks, n_chunks))

        ins = [pl.BlockSpec((1, length, c_xc), tok(0)),
               pl.BlockSpec((1, bt.shape[1], length), tok_t),
               pl.BlockSpec((1, length, LANES), tok(0)),
               pl.BlockSpec((1, length, q.shape[2]), tok(0)),
               pl.BlockSpec((1, kt.shape[1], length), tok_t),
               pl.BlockSpec((1, length, n_v), tok(4))]
        return ins, pl.BlockSpec((1, length, n_out), tok(0))

    in_f, out_f = specs(0)
    in_b, out_b = specs(1)
    act = jax.ShapeDtypeStruct((n_batch, t_len, n_out), BF16)
    tok_args = (xc, bt, dt_raw, q, kt, p)
    return pl.pallas_call(
        _l0_scan_kernel, grid=(n_batch, n_chunks),
        in_specs=in_f + in_b + [_const_spec(dt_bias.shape), _const_spec(a_log.shape),
                                _const_spec(ret_logit.shape)],
        out_specs=[out_f, out_b], out_shape=[act, act],
        scratch_shapes=[pltpu.VMEM((2 * SSD_GROUPS, SSD_STATE, SSD_HEADS // SSD_GROUPS * SSD_HEAD_DIM), F32),
                        pltpu.VMEM((2 * RET_HEADS, RET_QK_DIM, RET_V_DIM), F32)],
        compiler_params=_cparams(("arbitrary", "arbitrary")), name="l0_scan",
    )(*tok_args, *tok_args, dt_bias, a_log, ret_logit)


def _skew(j, n_total, tiles_per_batch):
    t_cur = jnp.minimum(j, n_total - 1)
    t_prev = jnp.maximum(j - 1, 0)
    return (lax.div(t_cur, tiles_per_batch), lax.rem(t_cur, tiles_per_batch),
            lax.div(t_prev, tiles_per_batch), lax.rem(t_prev, tiles_per_batch))


def _l0_out_kernel(yf_ref, yb_ref, xs_ref, z_ref, g_ref, ctx_ref, x_ref, mod_ref, dskip_ref, nw_ref,
                   ow_ref, fnw_ref, w1_ref, w3_ref, w2_ref, o_ref, x1_scr, *, n_batch, n_ctx_tiles, n_tiles,
                   d_model):
    j = pl.program_id(0)
    b, i, b_prev, i_prev = _skew(j, n_batch * n_tiles, n_tiles)
    slot = lax.rem(j, 2)

    @pl.when(j == 0)
    def _():
        x1_scr[...] = jnp.zeros_like(x1_scr)

    row_prev = jnp.where(i_prev < n_ctx_tiles, n_batch, b_prev)

    def write_out(val):
        o_ref[0] = val

    ffn = _ffn_stages(x1_scr[1 - slot], row_prev, mod_ref, fnw_ref, w1_ref, w3_ref, w2_ref, d_model,
                      write_out)

    is_ctx = i < n_ctx_tiles
    n_inner = SSD_HEADS * SSD_HEAD_DIM
    gsz = n_inner // SSD_GROUPS
    st = {"y": jnp.zeros((x_ref.shape[1], d_model), F32)}

    def ssd_group(g):
        sl = slice(g * gsz, (g + 1) * gsz)
        blk = (yf_ref[0, :, sl].astype(F32) + yb_ref[0, :, sl].astype(F32)
               + dskip_ref[:, sl] * xs_ref[0, :, sl].astype(F32))
        z = z_ref[0, :, sl].astype(F32)
        blk = blk * (z * jax.nn.sigmoid(z))
        piece = _rmsnorm(blk, nw_ref[:, sl]).astype(BF16)
        st["y"] = st["y"] + _dot(piece, ow_ref[sl, :])

    def ret_head(h):
        sl = slice(n_inner + h * RET_V_DIM, n_inner + (h + 1) * RET_V_DIM)
        blk = yf_ref[0, :, sl].astype(F32) + yb_ref[0, :, sl].astype(F32)
        dlt = blk - jnp.mean(blk, axis=-1, keepdims=True)
        nrm = dlt * lax.rsqrt(jnp.mean(dlt * dlt, axis=-1, keepdims=True) + LN_EPS)
        gate = g_ref[0, :, h * RET_V_DIM:(h + 1) * RET_V_DIM].astype(F32)
        piece = (nrm * (gate * jax.nn.sigmoid(gate))).astype(BF16)
        st["y"] = st["y"] + _dot(piece, ow_ref[sl, :])

    def residual():
        row = jnp.where(is_ctx, n_batch, b)
        gate_mix = mod_ref[pl.ds(row, 1), 2 * d_model:3 * d_model]
        xres = jnp.where(is_ctx, ctx_ref[0], x_ref[0])
        x1_scr[slot] = xres + gate_mix * st["y"]

    mixer = ([functools.partial(ssd_group, g) for g in range(SSD_GROUPS)]
             + [functools.partial(ret_head, h) for h in range(RET_HEADS)] + [residual])
    _run_interleaved(ffn, mixer)


def _l0_out(y_fwd, y_bwd, xbc, p, x, ctx, mod, d_skip_e, ssd_norm_w, out_w, ffn_norm_w, w1, w3, w2, *,
            n_ctx_tiles):
    n_batch, t_len, n_y = y_fwd.shape
    tm = ROW_TILE
    n_tiles = t_len // tm
    n_total = n_batch * n_tiles
    d_model = out_w.shape[1]
    n_inner = SSD_HEADS * SSD_HEAD_DIM
    kern = functools.partial(_l0_out_kernel, n_batch=n_batch, n_ctx_tiles=n_ctx_tiles, n_tiles=n_tiles,
                             d_model=d_model)

    def cur(fn):
        def index_map(j):
            b, i, _, _ = _skew(j, n_total, n_tiles)
            return fn(b, i)
        return index_map

    def prev(j):
        _, _, b, i = _skew(j, n_total, n_tiles)
        return (b, i, 0)

    in_specs = [
        pl.BlockSpec((1, tm, n_y), cur(lambda b, i: (b, i, 0))),
        pl.BlockSpec((1, tm, n_y), cur(lambda b, i: (b, i, 0))),
        pl.BlockSpec((1, tm, n_inner), cur(lambda b, i: (b, i, 0))),
        pl.BlockSpec((1, tm, n_inner), cur(lambda b, i: (b, i, 2))),
        pl.BlockSpec((1, tm, n_inner), cur(lambda b, i: (b, i, 5))),
        pl.BlockSpec((1, tm, d_model), cur(lambda b, i: (b, jnp.minimum(i, n_ctx_tiles - 1), 0))),
        pl.BlockSpec((1, tm, d_model), cur(lambda b, i: (b, jnp.maximum(i - n_ctx_tiles, 0), 0))),
        _const_spec(mod.shape), _const_spec((1, n_inner)), _const_spec((1, n_inner)),
        _const_spec(out_w.shape), _const_spec((1, d_model)),
        _const_spec(w1.shape), _const_spec(w3.shape), _const_spec(w2.shape)]
    return pl.pallas_call(
        kern, grid=(n_total + 1,), in_specs=in_specs,
        out_specs=pl.BlockSpec((1, tm, d_model), prev),
        out_shape=jax.ShapeDtypeStruct((n_batch, t_len, d_model), F32),
        scratch_shapes=[pltpu.VMEM((2, tm, d_model), F32)],
        compiler_params=_cparams(("arbitrary",)), name="l0_out_ffn",
    )(y_fwd, y_bwd, xbc, p, p, ctx, x, mod, d_skip_e.reshape(1, n_inner), ssd_norm_w.reshape(1, n_inner), out_w,
      ffn_norm_w.reshape(1, d_model), w1, w3, w2)


FFN_CHUNK = 256


def _ffn_stages(x, row, mod_ref, nw_ref, w1_ref, w3_ref, w2_ref, d_model, finish):
    st = {}

    def start():
        shift = mod_ref[pl.ds(row, 1), 3 * d_model:4 * d_model]
        scale = mod_ref[pl.ds(row, 1), 4 * d_model:5 * d_model]
        st["hb"] = (_rmsnorm(x, nw_ref[...]) * (1.0 + scale) + shift).astype(BF16)

    def up_gate():
        a = _dot(st["hb"], w1_ref[...])
        st["a"] = a * jax.nn.sigmoid(a)

    def up_lin():
        st["t"] = (st["a"] * _dot(st["hb"], w3_ref[...])).astype(BF16)

    def down():
        gate = mod_ref[pl.ds(row, 1), 5 * d_model:6 * d_model]
        finish(x + gate * _dot(st["t"], w2_ref[...]))

    return [start, up_gate, up_lin, down]


def _run_interleaved(first, second):
    n1, n2 = len(first), len(second)
    done = 0
    first[0]()
    for k in range(1, n1):
        first[k]()
        upto = -(-k * n2 // (n1 - 1))
        for s in second[done:upto]:
            s()
        done = upto


def _ml_prep_kernel(prev_ref, cur_ref, next_ref, cw_ref, cb_ref, wq_ref, wk_ref, wv_ref, gw_ref, gb_ref,
                    q_ref, k_ref, kt_ref, v_ref, xc_ref, gates_ref, *, n_ctx_tiles, n_tiles, k_scale):
    i = pl.program_id(1)
    prev_ok, next_ok = _halo_flags(i, n_ctx_tiles, n_tiles)
    n_blocks, blk, _ = wq_ref.shape
    tm = cur_ref.shape[1]
    gates = jnp.zeros((tm, LANES), F32) + gb_ref[...]
    for c in range(n_blocks):
        sl = slice(c * blk, (c + 1) * blk)
        xm = cur_ref[0, :, sl]
        xc = _conv_silu(prev_ref[0, :, sl], xm, next_ref[0, :, sl], prev_ok, next_ok,
                        cw_ref[:, sl], cb_ref[:, sl]).astype(BF16)
        xc_ref[0, :, sl] = xc
        q = _dot(xc, wq_ref[c]).astype(BF16)
        k = _dot(xc, wk_ref[c])
        v = _dot(xm, wv_ref[c]).astype(BF16)
        q_ref[0, :, sl] = q
        ks = k * k_scale
        k_ref[0, :, sl] = ks.astype(BF16)
        kt_ref[0, sl, :] = ks.T.astype(BF16)
        v_ref[0, :, sl] = v
        gates = gates + _dot(q, gw_ref[0, c]) + _dot(k.astype(BF16), gw_ref[1, c]) + _dot(v, gw_ref[2, c])
    gates_ref[0] = gates


def _ml_prep(u, conv_w, conv_b, wq_bd, wk_bd, wv_bd, gate_w3, gate_b, *, n_ctx_tiles, k_scale):
    n_batch, t_len, _ = u.shape
    tm = ROW_TILE
    n_tiles = t_len // tm
    inner = conv_w.shape[1]
    kern = functools.partial(_ml_prep_kernel, n_ctx_tiles=n_ctx_tiles, n_tiles=n_tiles, k_scale=k_scale)
    tok = pl.BlockSpec((1, tm, inner), lambda b, i: (b, i, 0))
    in_specs = _halo_specs(tm, inner, 0, t_len) + [
        _const_spec(conv_w.shape), _const_spec((1, inner)),
        _const_spec(wq_bd.shape), _const_spec(wk_bd.shape), _const_spec(wv_bd.shape),
        _const_spec(gate_w3.shape), _const_spec((1, LANES))]
    act = jax.ShapeDtypeStruct((n_batch, t_len, inner), BF16)
    return pl.pallas_call(
        kern, grid=(n_batch, n_tiles), in_specs=in_specs,
        out_specs=[tok, tok, pl.BlockSpec((1, inner, tm), lambda b, i: (b, 0, i)), tok, tok,
                   pl.BlockSpec((1, tm, LANES), lambda b, i: (b, i, 0))],
        out_shape=[act, act, jax.ShapeDtypeStruct((n_batch, inner, t_len), BF16), act, act,
                   jax.ShapeDtypeStruct((n_batch, t_len, LANES), F32)],
        compiler_params=_cparams(("parallel", "parallel")), name="ml_conv_qkv_gates",
    )(u, u, u, conv_w, conv_b.reshape(1, inner), wq_bd, wk_bd, wv_bd, gate_w3, gate_b)


def _ml_scan_direction(direction, q_ref, k_ref, kt_ref, v_ref, g_ref, h_ref, c_st, vc_st, n_st, m_st):
    length = q_ref.shape[1]
    dh = q_ref.shape[2] // MLSTM_HEADS
    _, mask = _scan_masks(length, direction)
    maskf = mask.astype(F32)
    neg_inf = jnp.float32(-jnp.inf)
    gt = g_ref[0]
    gsel = gt if direction == 0 else pltpu.roll(gt, LANES - 2 * MLSTM_HEADS, 1)
    lf = jax.nn.log_sigmoid(gsel) * LOG2E
    ipre = gsel * LOG2E
    bcum = jnp.dot(maskf, lf, preferred_element_type=F32, precision=HIGHEST)
    btot = jnp.sum(lf, axis=0, keepdims=True)
    bcum_t = bcum.T
    ipre_t = ipre.T
    for h in range(MLSTM_HEADS):
        hs = direction * MLSTM_HEADS + h
        fl = MLSTM_HEADS + h
        bc_col = jnp.broadcast_to(bcum[:, fl:fl + 1], (length, LANES))
        bc_row = bcum_t[fl:fl + 1, :]
        ic_row = ipre_t[h:h + 1, :]
        m_prev = m_st[hs:hs + 1, :]
        dmat = jnp.where(mask, jnp.concatenate([bc_col] * (length // LANES), axis=1) + (ic_row - bc_row),
                         neg_inf)
        prev = bc_col + m_prev
        m_out = jnp.maximum(prev, jnp.max(dmat, axis=1, keepdims=True))
        sl = slice(h * dh, (h + 1) * dh)
        q_h = q_ref[0, :, sl]
        kt_h = kt_ref[0, sl, :]
        m_out_w = jnp.concatenate([m_out] * (length // LANES), axis=1)
        sm = _dot(q_h, kt_h) * jnp.exp2(dmat - m_out_w)
        w_prev = jnp.exp2(prev - m_out)
        n_prev = n_st[hs:hs + 1, :]
        vc_st[hs, 0:length, :] = v_ref[0, :, sl]
        qw = q_h * jnp.concatenate([w_prev.astype(BF16)] * (dh // LANES), axis=1)
        num = _dot(jnp.concatenate([sm.astype(BF16), qw], axis=1), vc_st[hs])
        qn = q_h * n_prev.astype(BF16)
        qn = (qn[:, 0:LANES] + qn[:, LANES:2 * LANES]) + (qn[:, 2 * LANES:3 * LANES] + qn[:, 3 * LANES:])
        den = (jnp.sum(sm, axis=1, keepdims=True)
               + w_prev * jnp.sum(qn.astype(F32), axis=1, keepdims=True))
        inv = 1.0 / jnp.maximum(jnp.abs(den), jnp.exp2(-m_out))
        h_ref[0, :, sl] = (num * jnp.concatenate([inv] * (dh // LANES), axis=1)).astype(BF16)
        b_end = btot[:, fl:fl + 1]
        d_end_row = b_end - bc_row + ic_row
        m_new = jnp.maximum(b_end + m_prev, jnp.max(d_end_row, axis=1, keepdims=True))
        w_row = jnp.exp2(d_end_row - m_new[:, 0:1])
        w_row_b = jnp.broadcast_to(w_row, (16, length)).astype(BF16)
        kt_w = kt_h * jnp.concatenate([w_row_b] * (dh // 16), axis=0)
        w_c = jnp.exp2(b_end + m_prev - m_new)
        c_new = w_c[:, 0:1] * c_st[hs] + _dot(kt_w, v_ref[0, :, sl])
        c_st[hs] = c_new
        vc_st[hs, length:, :] = c_new.astype(BF16)
        n_inc = _dot(w_row_b, k_ref[0, :, sl])
        n_st[hs:hs + 1, :] = w_c[:, 0:1] * n_prev + n_inc[0:1, :]
        m_st[hs:hs + 1, :] = m_new


def _ml_scan_kernel(qf, kf, ktf, vf, gf, qb, kb, ktb, vb, gb, hf_ref, hb_ref, c_st, vc_st, n_st, m_st):
    @pl.when(pl.program_id(1) == 0)
    def _():
        c_st[...] = jnp.zeros_like(c_st)
        vc_st[...] = jnp.zeros_like(vc_st)
        n_st[...] = jnp.zeros_like(n_st)
        m_st[...] = jnp.zeros_like(m_st)

    _ml_scan_direction(0, qf, kf, ktf, vf, gf, hf_ref, c_st, vc_st, n_st, m_st)
    _ml_scan_direction(1, qb, kb, ktb, vb, gb, hb_ref, c_st, vc_st, n_st, m_st)


def _ml_scan(q, k, kt, v, gates, *, n_ctx_chunks):
    n_batch, t_len, inner = q.shape
    length = ML_CHUNK
    n_chunks = t_len // length
    dh = inner // MLSTM_HEADS
    n_slots = 2 * MLSTM_HEADS

    def specs(direction):
        def tok(b, s):
            return (b, _scan_chunk_index(direction, s, n_ctx_chunks, n_chunks), 0)

        def tok_t(b, s):
            return (b, 0, _scan_chunk_index(direction, s, n_ctx_chunks, n_chunks))

        ins = [pl.BlockSpec((1, length, inner), tok), pl.BlockSpec((1, length, inner), tok),
               pl.BlockSpec((1, inner, length), tok_t), pl.BlockSpec((1, length, inner), tok),
               pl.BlockSpec((1, length, LANES), tok)]
        return ins, pl.BlockSpec((1, length, inner), tok)

    in_f, out_f = specs(0)
    in_b, out_b = specs(1)
    act = jax.ShapeDtypeStruct((n_batch, t_len, inner), BF16)
    return pl.pallas_call(
        _ml_scan_kernel, grid=(n_batch, n_chunks),
        in_specs=in_f + in_b, out_specs=[out_f, out_b], out_shape=[act, act],
        scratch_shapes=[pltpu.VMEM((n_slots, dh, dh), F32), pltpu.VMEM((n_slots, length + dh, dh), BF16),
                        pltpu.VMEM((n_slots, dh), F32), pltpu.VMEM((n_slots, LANES), F32)],
        compiler_params=_cparams(("arbitrary", "arbitrary")), name="ml_scan",
    )(q, k, kt, v, gates, q, k, kt, v, gates)


def _ml_out_kernel(hf_ref, hb_ref, xc_ref, z_ref, x_ref, mod_ref, nw_ref, skip_ref, dw_ref,
                   fnw_ref, w1_ref, w3_ref, w2_ref, finw_ref, o_ref, x2_scr, *, n_total, n_tiles, d_model):
    j = pl.program_id(0)
    b, _, b_prev, _ = _skew(j, n_total, n_tiles)
    slot = lax.rem(j, 2)

    @pl.when(j == 0)
    def _():
        x2_scr[...] = jnp.zeros_like(x2_scr)

    def write_out(val):
        o_ref[0] = _rmsnorm(val, finw_ref[...])

    ffn = _ffn_stages(x2_scr[1 - slot], b_prev, mod_ref, fnw_ref, w1_ref, w3_ref, w2_ref, d_model,
                      write_out)

    inner = hf_ref.shape[2]
    dh = inner // MLSTM_HEADS
    half = dh // 2
    st = {"y": jnp.zeros((x_ref.shape[1], d_model), F32)}

    def head_stats(h):
        sl = slice(h * dh, (h + 1) * dh)
        blk = hf_ref[0, :, sl].astype(F32) + hb_ref[0, :, sl].astype(F32)
        mu = jnp.mean(blk, axis=-1, keepdims=True)
        dlt = blk - mu
        st["mu"] = mu
        st["rs"] = lax.rsqrt(jnp.mean(dlt * dlt, axis=-1, keepdims=True) + LN_EPS)

    def head_half(h, part):
        sl = slice(h * dh + part * half, h * dh + (part + 1) * half)
        blk = hf_ref[0, :, sl].astype(F32) + hb_ref[0, :, sl].astype(F32)
        nrm = (blk - st["mu"]) * st["rs"] * nw_ref[:, sl]
        z = z_ref[0, :, sl].astype(F32)
        piece = ((nrm + skip_ref[:, sl] * xc_ref[0, :, sl].astype(F32)) * (z * jax.nn.sigmoid(z))).astype(BF16)
        st["y"] = st["y"] + _dot(piece, dw_ref[sl, :])

    def residual():
        gate_mix = mod_ref[pl.ds(b, 1), 2 * d_model:3 * d_model]
        x2_scr[slot] = x_ref[0] + gate_mix * st["y"]

    mixer = []
    for h in range(MLSTM_HEADS):
        mixer += [functools.partial(head_stats, h), functools.partial(head_half, h, 0),
                  functools.partial(head_half, h, 1)]
    _run_interleaved(ffn, mixer + [residual])


def _ml_out(h_fwd, h_bwd, xc, u, x1, mod, norm_w, skip, down_w, ffn_norm_w, w1, w3, w2, final_w, *,
            tile_offset):
    n_batch, t_len, inner = h_fwd.shape
    tm = ROW_TILE
    n_tiles = t_len // tm - tile_offset
    n_total = n_batch * n_tiles
    d_model = down_w.shape[1]
    kern = functools.partial(_ml_out_kernel, n_total=n_total, n_tiles=n_tiles, d_model=d_model)
    off = tile_offset

    def cur(col_block):
        def index_map(j):
            b, i, _, _ = _skew(j, n_total, n_tiles)
            return (b, i + off, col_block)
        return index_map

    def prev(j):
        _, _, b, i = _skew(j, n_total, n_tiles)
        return (b, i, 0)

    in_specs = [
        pl.BlockSpec((1, tm, inner), cur(0)),
        pl.BlockSpec((1, tm, inner), cur(0)),
        pl.BlockSpec((1, tm, inner), cur(0)),
        pl.BlockSpec((1, tm, inner), cur(1)),
        pl.BlockSpec((1, tm, d_model), cur(0)),
        _const_spec(mod.shape), _const_spec((1, inner)), _const_spec((1, inner)), _const_spec(down_w.shape),
        _const_spec((1, d_model)), _const_spec(w1.shape), _const_spec(w3.shape), _const_spec(w2.shape),
        _const_spec((1, d_model))]
    return pl.pallas_call(
        kern, grid=(n_total + 1,), in_specs=in_specs,
        out_specs=pl.BlockSpec((1, tm, d_model), prev),
        out_shape=jax.ShapeDtypeStruct((n_batch, n_tiles * tm, d_model), F32),
        scratch_shapes=[pltpu.VMEM((2, tm, d_model), F32)],
        compiler_params=_cparams(("arbitrary",)), name="ml_out_ffn",
    )(h_fwd, h_bwd, xc, u, x1, mod, norm_w.reshape(1, inner), skip.reshape(1, inner), down_w,
      ffn_norm_w.reshape(1, d_model), w1, w3, w2, final_w.reshape(1, d_model))


def _rope_tables(n_ctx, n_lat):
    rows = n_lat // GRID_W
    r = jnp.repeat(jnp.arange(rows, dtype=F32), GRID_W)
    col = jnp.tile(jnp.arange(GRID_W, dtype=F32), rows)
    nf = RET_QK_DIM // 4
    inv = ROPE_BASE ** (-jnp.arange(nf, dtype=F32) / nf)
    ang = jnp.concatenate([r[:, None] * inv, col[:, None] * inv], axis=-1)
    cos = jnp.concatenate([jnp.ones((n_ctx, 2 * nf), F32), jnp.cos(ang)], axis=0)
    sin = jnp.concatenate([jnp.zeros((n_ctx, 2 * nf), F32), jnp.sin(ang)], axis=0)
    return jnp.concatenate([cos, cos], axis=1), jnp.concatenate([-sin, sin], axis=1)


def _pad_lanes(v):
    return jnp.pad(v, [(0, 0)] * (v.ndim - 1) + [(0, LANES - v.shape[-1])])


def _block_diag(w, blocks_per_tile):
    n_k = w.shape[0]
    w4 = w.reshape(n_k // blocks_per_tile, blocks_per_tile, QKV_BLOCK, QKV_BLOCK)
    eye = jnp.eye(blocks_per_tile, dtype=w.dtype)
    side = blocks_per_tile * QKV_BLOCK
    return jnp.einsum('ckde,kl->ckdle', w4, eye).reshape(n_k // blocks_per_tile, side, side)


def kernel(x, c, ctx, c_ctx, ada_w, ada_b, norm_mix_w, norm_ffn_w, ffn_w1, ffn_w3, ffn_w2, ab_in_w, ab_conv_w, ab_conv_b, ssd_dt_bias_f, ssd_dt_bias_b, ssd_a_log_f, ssd_a_log_b, ssd_d, ssd_norm_w, ret_logit_f, ret_logit_b, ab_out_w, ml_up_w, ml_conv_w, ml_conv_b, ml_wq, ml_wk, ml_wv, ml_gate_w, ml_gate_b, ml_norm_w, ml_skip, ml_down_w, final_norm_w):
    n_batch, n_lat, d_model = x.shape
    n_ctx = ctx.shape[1]
    t_len = n_ctx + n_lat
    tm = ROW_TILE
    assert n_ctx % tm == 0 and n_lat % tm == 0 and n_ctx % ML_CHUNK == 0 and n_batch + 1 <= 8
    n_ctx_tiles = n_ctx // tm
    n_tiles = t_len // tm

    cvecs = jnp.concatenate([c, c_ctx[None, :], jnp.zeros((8 - n_batch - 1, d_model), F32)], axis=0)
    mod = _modulation(cvecs, ada_w, ada_b)

    n_inner = SSD_HEADS * SSD_HEAD_DIM
    n_conv = n_inner + 2 * SSD_GROUPS * SSD_STATE
    n_qk = RET_HEADS * RET_QK_DIM
    n_v = RET_HEADS * RET_V_DIM
    w_in = ab_in_w[0]
    o_z, o_xbc, o_dt = 0, n_inner, n_inner + n_conv
    o_q = o_dt + SSD_HEADS
    o_k, o_v, o_g = o_q + n_qk, o_q + 2 * n_qk, o_q + 2 * n_qk + n_v
    w_main = jnp.concatenate([w_in[:, o_xbc:o_dt], w_in[:, o_z:o_xbc], w_in[:, o_q:o_k], w_in[:, o_k:o_v],
                              w_in[:, o_v:o_g], w_in[:, o_g:o_g + n_v]], axis=1).astype(BF16)
    w_dt = _pad_lanes(w_in[:, o_dt:o_q]).astype(BF16)
    p, dt_raw = _projection(x, ctx, mod[0], norm_mix_w[0], w_main, w_dt, n_batch=n_batch,
                            n_ctx_tiles=n_ctx_tiles, n_tiles=n_tiles, name="l0_in_proj")
    cc, ss = _rope_tables(n_ctx, n_lat)
    xcc, bt, q0, kt0 = _l0_prep(p, cc, ss, ab_conv_w[0], ab_conv_b[0], n_ctx_tiles=n_ctx_tiles)
    dt_bias = _pad_lanes(jnp.stack([ssd_dt_bias_f[0], ssd_dt_bias_b[0]]))[:, None, :]
    a_log = _pad_lanes(jnp.stack([ssd_a_log_f[0], ssd_a_log_b[0]]))[:, None, :]
    r_logit = _pad_lanes(jnp.stack([ret_logit_f[0], ret_logit_b[0]]))[:, None, :]
    y_fwd, y_bwd = _l0_scan(xcc, bt, dt_raw, q0, kt0, p, dt_bias, a_log, r_logit,
                            n_ctx_chunks=n_ctx // SSD_CHUNK)
    d_skip_e = jnp.repeat(ssd_d[0], SSD_HEAD_DIM)
    ffn_w1b, ffn_w3b, ffn_w2b = ffn_w1.astype(BF16), ffn_w3.astype(BF16), ffn_w2.astype(BF16)
    x1 = _l0_out(y_fwd, y_bwd, xcc, p, x, ctx, mod[0], d_skip_e, ssd_norm_w[0], ab_out_w[0].astype(BF16),
                 norm_ffn_w[0], ffn_w1b[0], ffn_w3b[0], ffn_w2b[0], n_ctx_tiles=n_ctx_tiles)

    inner = ml_conv_w.shape[2]
    (u,) = _projection(x1, None, mod[1], norm_mix_w[1], ml_up_w[0].astype(BF16), None, n_batch=n_batch,
                       n_ctx_tiles=n_ctx_tiles, n_tiles=n_tiles, name="ml_up_proj")
    bpt = 256 // QKV_BLOCK
    gate_w3 = _pad_lanes(ml_gate_w[0]).reshape(3, inner // 256, 256, LANES).astype(BF16)
    q, k, kt, v, xc, gates = _ml_prep(
        u, ml_conv_w[0], ml_conv_b[0], _block_diag(ml_wq[0], bpt).astype(BF16),
        _block_diag(ml_wk[0], bpt).astype(BF16), _block_diag(ml_wv[0], bpt).astype(BF16), gate_w3,
        _pad_lanes(ml_gate_b[0])[None, :], n_ctx_tiles=n_ctx_tiles,
        k_scale=float(inner // MLSTM_HEADS) ** -0.5)
    h_fwd, h_bwd = _ml_scan(q, k, kt, v, gates, n_ctx_chunks=n_ctx // ML_CHUNK)
    return _ml_out(h_fwd, h_bwd, xc, u, x1, mod[1], ml_norm_w[0], ml_skip[0], ml_down_w[0].astype(BF16),
                   norm_ffn_w[1], ffn_w1b[1], ffn_w3b[1], ffn_w2b[1], final_norm_w, tile_offset=n_ctx_tiles)
```
